```python
import math
import jax
import jax.numpy as jnp
from jax import lax
import numpy as np

D_MODEL = 1024
BATCH = 4
SEQ = 8192
DEPTH = 4

GRID_W = 64
CTX_LEN = 256
N_MIXERS = 2
ATTN_MIXER = 0
SGU_MIXER = 1
N_ATTN_LAYERS = (DEPTH + 1) // 2
N_SGU_LAYERS = DEPTH // 2
N_MOD = 6
EPS = 1e-6

DA_HEADS = 8
DA_HEAD_DIM = 64
DA_QK_WIDTH = DA_HEADS * 2 * DA_HEAD_DIM
DA_V_WIDTH = DA_HEADS * 2 * DA_HEAD_DIM
Q_BLOCK = 128
ROPE_THETA = 10000.0
ROPE_AXIS_DIM = DA_HEAD_DIM // 2
ROPE_PAIRS = ROPE_AXIS_DIM // 2

SGU_WIDTH = 4 * D_MODEL
SGU_HALF = SGU_WIDTH // 2
SGU_GROUPS = 8
CHUNK = 128

MLP_HIDDEN = 4 * D_MODEL

kernel_name = 'hybrid_diffattn_sgu_dit'


def rms_norm(x, g):
    xf = x.astype(jnp.float32)
    y = xf * lax.rsqrt(jnp.mean(xf * xf, axis=-1, keepdims=True) + EPS)
    return (y * g.astype(jnp.float32)).astype(x.dtype)


def layer_norm(x, g, b):
    xf = x.astype(jnp.float32)
    mu = jnp.mean(xf, axis=-1, keepdims=True)
    var = jnp.mean(jnp.square(xf - mu), axis=-1, keepdims=True)
    y = (xf - mu) * lax.rsqrt(var + EPS)
    return (y * g.astype(jnp.float32) + b.astype(jnp.float32)).astype(x.dtype)


def modulate(h, shift, scale):
    return h * (1 + scale) + shift


def axial_rope_tables(n_rows):
    rows = jnp.broadcast_to(jnp.arange(n_rows, dtype=jnp.float32)[:, None], (n_rows, GRID_W)).reshape(-1)
    cols = jnp.broadcast_to(jnp.arange(GRID_W, dtype=jnp.float32)[None, :], (n_rows, GRID_W)).reshape(-1)
    inv_freq = ROPE_THETA ** (-jnp.arange(ROPE_PAIRS, dtype=jnp.float32) / ROPE_PAIRS)
    ang_r = rows[:, None] * inv_freq
    ang_c = cols[:, None] * inv_freq
    ang = jnp.concatenate([ang_r, ang_r, ang_c, ang_c], axis=-1)
    return jnp.cos(ang), jnp.sin(ang)


def apply_rope(x, cos, sin):
    xs = x.reshape(x.shape[:-1] + (2, 2, ROPE_PAIRS))
    rot = jnp.stack([-xs[..., 1, :], xs[..., 0, :]], axis=-2).reshape(x.shape)
    c = cos[:, None, None, :].astype(x.dtype)
    s = sin[:, None, None, :].astype(x.dtype)
    return x * c + rot * s


def diff_attend(q, k, v, lam):
    s = jnp.einsum('bqhmd,bkhmd->mbhqk', q, k).astype(jnp.float32) * (DA_HEAD_DIM ** -0.5)
    p = jax.nn.softmax(s, axis=-1)
    a = (p[0] - lam * p[1]).astype(v.dtype)
    return jnp.einsum('bhqk,bkhe->bqhe', a, v)


def diff_heads_out(o, subln_g, lam_init, w_o):
    o = rms_norm(o, subln_g) * (1 - lam_init)
    return o.reshape(o.shape[0], o.shape[1], DA_V_WIDTH) @ w_o


def differential_attention(h_lat, h_ctx, w_qkv, w_o, lam_params, subln_g, lam_init, cos, sin, ctx_queries):
    B, S, _ = h_lat.shape
    C = h_ctx.shape[1]
    qkv = h_lat @ w_qkv
    q = apply_rope(qkv[..., :DA_QK_WIDTH].reshape(B, S, DA_HEADS, 2, DA_HEAD_DIM), cos, sin)
    k = apply_rope(qkv[..., DA_QK_WIDTH:2 * DA_QK_WIDTH].reshape(B, S, DA_HEADS, 2, DA_HEAD_DIM), cos, sin)
    v = qkv[..., 2 * DA_QK_WIDTH:].reshape(B, S, DA_HEADS, 2 * DA_HEAD_DIM)
    kv_c = h_ctx @ w_qkv[:, DA_QK_WIDTH:]
    k_c = kv_c[..., :DA_QK_WIDTH].reshape(B, C, DA_HEADS, 2, DA_HEAD_DIM)
    v_c = kv_c[..., DA_QK_WIDTH:].reshape(B, C, DA_HEADS, 2 * DA_HEAD_DIM)
    lp = lam_params.astype(jnp.float32)
    lam = jnp.exp(jnp.sum(lp[0] * lp[1])) - jnp.exp(jnp.sum(lp[2] * lp[3])) + lam_init
    k_all = jnp.concatenate([k_c, k], axis=1)
    v_all = jnp.concatenate([v_c, v], axis=1)
    n_blk = S // Q_BLOCK
    q_blocks = jnp.moveaxis(q.reshape(B, n_blk, Q_BLOCK, DA_HEADS, 2, DA_HEAD_DIM), 1, 0)
    o = lax.map(lambda qb: diff_attend(qb, k_all, v_all, lam), q_blocks)
    o = jnp.moveaxis(o, 0, 1).reshape(B, S, DA_HEADS, 2 * DA_HEAD_DIM)
    y_lat = diff_heads_out(o, subln_g, lam_init, w_o)
    y_ctx = None
    if ctx_queries:
        q_c = (h_ctx @ w_qkv[:, :DA_QK_WIDTH]).reshape(B, C, DA_HEADS, 2, DA_HEAD_DIM)
        y_ctx = diff_heads_out(diff_attend(q_c, k_c, v_c, lam), subln_g, lam_init, w_o)
    return y_lat, y_ctx


def spatial_gating_mix(h, w_in, b_in, ln_g, ln_b, w_s, b_s, w_out):
    B, N, _ = h.shape
    z = jax.nn.gelu(h @ w_in + b_in, approximate=False)
    u = z[..., :SGU_HALF]
    v = layer_norm(z[..., SGU_HALF:], ln_g, ln_b)
    v = v.reshape(B, N // CHUNK, CHUNK, SGU_GROUPS, SGU_HALF // SGU_GROUPS)
    sv = jnp.einsum('gpq,bnqgc->bnpgc', w_s, v) + b_s.T[:, :, None]
    return (u * sv.reshape(B, N, SGU_HALF)) @ w_out


def channel_mlp(h, w1, w2):
    return jnp.square(jax.nn.relu(h @ w1)) @ w2


def setup_inputs(seed: int = 0) -> dict:
    key = jax.random.key(seed)
    ks = jax.random.split(key, 24)
    f32 = jnp.float32
    D = D_MODEL

    def nrm(k, shape, scale):
        return jax.random.normal(k, shape, f32) * scale

    return {
        'x': nrm(ks[0], (BATCH, SEQ, D), 1.0),
        'c': nrm(ks[1], (BATCH, D), 1.0),
        'ctx': nrm(ks[2], (BATCH, CTX_LEN, D), 1.0),
        'c_ctx': nrm(ks[3], (D,), 1.0),
        'w_mod': nrm(ks[4], (DEPTH, D, N_MOD * D), 0.5 * D ** -0.5),
        'b_mod': nrm(ks[5], (DEPTH, N_MOD * D), 0.02),
        'norm_g': 1.0 + nrm(ks[6], (DEPTH, 4, D), 0.02),
        'da_w_qkv': nrm(ks[7], (N_ATTN_LAYERS, D, 2 * DA_QK_WIDTH + DA_V_WIDTH), D ** -0.5),
        'da_w_o': nrm(ks[8], (N_ATTN_LAYERS, DA_V_WIDTH, D), DA_V_WIDTH ** -0.5),
        'da_lambda': nrm(ks[9], (N_ATTN_LAYERS, 4, DA_HEAD_DIM), 0.1),
        'da_subln_g': 1.0 + nrm(ks[10], (N_ATTN_LAYERS, 2 * DA_HEAD_DIM), 0.02),
        'sgu_w_in': nrm(ks[11], (N_SGU_LAYERS, D, SGU_WIDTH), D ** -0.5),
        'sgu_b_in': nrm(ks[12], (N_SGU_LAYERS, SGU_WIDTH), 0.02),
        'sgu_ln_g': 1.0 + nrm(ks[13], (N_SGU_LAYERS, SGU_HALF), 0.02),
        'sgu_ln_b': nrm(ks[14], (N_SGU_LAYERS, SGU_HALF), 0.02),
        'sgu_w_s': nrm(ks[15], (N_SGU_LAYERS, SGU_GROUPS, CHUNK, CHUNK), CHUNK ** -0.5),
        'sgu_b_s': 1.0 + nrm(ks[16], (N_SGU_LAYERS, SGU_GROUPS, CHUNK), 0.02),
        'sgu_w_out': nrm(ks[17], (N_SGU_LAYERS, SGU_HALF, D), SGU_HALF ** -0.5),
        'mlp_w1': nrm(ks[18], (DEPTH, D, MLP_HIDDEN), D ** -0.5),
        'mlp_w2': nrm(ks[19], (DEPTH, MLP_HIDDEN, D), MLP_HIDDEN ** -0.5),
    }


def reference(x, c, ctx, c_ctx, w_mod, b_mod, norm_g, da_w_qkv, da_w_o, da_lambda, da_subln_g,
              sgu_w_in, sgu_b_in, sgu_ln_g, sgu_ln_b, sgu_w_s, sgu_b_s, sgu_w_out, mlp_w1, mlp_w2):
    B, S, D = x.shape
    ROWS = S // GRID_W
    cos, sin = axial_rope_tables(ROWS)
    silu_c = jax.nn.silu(c)
    silu_cc = jax.nn.silu(c_ctx)
    h_ctx = ctx
    for i in range(DEPTH):
        mixer = i % N_MIXERS
        j = i // N_MIXERS
        ctx_after = any((l % N_MIXERS) == ATTN_MIXER for l in range(i + 1, DEPTH))
        ctx_in = (mixer == ATTN_MIXER) or ctx_after
        mod = (silu_c @ w_mod[i] + b_mod[i]).reshape(B, N_MOD, 1, D)
        sh_a, sc_a, gt_a, sh_f, sc_f, gt_f = [mod[:, m] for m in range(N_MOD)]
        g_pre_a, g_post_a, g_pre_f, g_post_f = [norm_g[i, m] for m in range(4)]
        a_lat = modulate(rms_norm(x, g_pre_a), sh_a, sc_a)
        if ctx_in:
            mod_c = (silu_cc @ w_mod[i] + b_mod[i]).reshape(N_MOD, 1, D)
            csh_a, csc_a, cgt_a, csh_f, csc_f, cgt_f = [mod_c[m] for m in range(N_MOD)]
            a_ctx = modulate(rms_norm(h_ctx, g_pre_a), csh_a, csc_a)
        if mixer == ATTN_MIXER:
            lam_init = 0.8 - 0.6 * math.exp(-0.3 * i)
            y_lat, y_ctx = differential_attention(a_lat, a_ctx, da_w_qkv[j], da_w_o[j], da_lambda[j],
                                                  da_subln_g[j], lam_init, cos, sin, ctx_after)
        else:
            y_lat = spatial_gating_mix(a_lat, sgu_w_in[j], sgu_b_in[j], sgu_ln_g[j], sgu_ln_b[j],
                                       sgu_w_s[j], sgu_b_s[j], sgu_w_out[j])
            y_ctx = None
            if ctx_after:
                y_ctx = spatial_gating_mix(a_ctx, sgu_w_in[j], sgu_b_in[j], sgu_ln_g[j], sgu_ln_b[j],
                                           sgu_w_s[j], sgu_b_s[j], sgu_w_out[j])
        x = x + gt_a * rms_norm(y_lat, g_post_a)
        f_lat = channel_mlp(modulate(rms_norm(x, g_pre_f), sh_f, sc_f), mlp_w1[i], mlp_w2[i])
        x = x + gt_f * rms_norm(f_lat, g_post_f)
        if ctx_after:
            h_ctx = h_ctx + cgt_a * rms_norm(y_ctx, g_post_a)
            f_ctx = channel_mlp(modulate(rms_norm(h_ctx, g_pre_f), csh_f, csc_f), mlp_w1[i], mlp_w2[i])
            h_ctx = h_ctx + cgt_f * rms_norm(f_ctx, g_post_f)
    return x
```

```python
import functools
import math

import jax
import jax.numpy as jnp
from jax import lax
from jax.experimental import pallas as pl
from jax.experimental.pallas import tpu as pltpu

D_MODEL = 1024
DEPTH = 4
GRID_W = 64
N_MOD = 6
EPS = 1e-6
DA_HEADS = 8
DA_HEAD_DIM = 64
HEAD_W = 2 * DA_HEAD_DIM
ROPE_THETA = 10000.0
ROPE_PAIRS = 16
SGU_HALF = 2048
SGU_GROUPS = 8
SGU_GROUP_W = SGU_HALF // SGU_GROUPS
CHUNK = 128
MLP_HIDDEN = 4 * D_MODEL

TM = 512
TQ = 256
TKV = 512
MOD_ROWS = 8
MOD_TN = 1536
HID_TN = 1024
V7X_VMEM_LIMIT = 56 * 1024 * 1024
Q_SCALE = DA_HEAD_DIM ** -0.5 * math.log2(math.e)

BF16 = jnp.bfloat16
F32 = jnp.float32


def _dot(a, b):
    return jnp.dot(a, b, preferred_element_type=F32)


def _rms(x, g):
    return x * lax.rsqrt(jnp.mean(x * x, axis=-1, keepdims=True) + EPS) * g


def _resident(shape):
    return pl.BlockSpec(shape, lambda *_: (0,) * len(shape), pipeline_mode=pl.Buffered(1))


def _mod_kernel(cv_ref, w_ref, b_ref, o_ref):
    cv = cv_ref[...]
    s = cv / (1.0 + jnp.exp(-cv))
    o_ref[...] = _dot(s.astype(BF16), w_ref[...].astype(BF16)) + b_ref[...]


def _modulation(cvec, w_mod, b_mod):
    n = N_MOD * D_MODEL
    out = pl.pallas_call(
        _mod_kernel,
        out_shape=jax.ShapeDtypeStruct((DEPTH, MOD_ROWS, n), F32),
        grid=(DEPTH, n // MOD_TN),
        in_specs=[
            pl.BlockSpec((MOD_ROWS, D_MODEL), lambda i, j: (0, 0)),
            pl.BlockSpec((None, D_MODEL, MOD_TN), lambda i, j: (i, 0, j)),
            pl.BlockSpec((None, 1, MOD_TN), lambda i, j: (i, 0, j)),
        ],
        out_specs=pl.BlockSpec((None, MOD_ROWS, MOD_TN), lambda i, j: (i, 0, j)),
        compiler_params=pltpu.CompilerParams(
            dimension_semantics=("parallel", "parallel"), vmem_limit_bytes=V7X_VMEM_LIMIT),
        name="modulation",
    )(cvec, w_mod, b_mod.reshape(DEPTH, 1, n))
    return out.reshape(DEPTH, MOD_ROWS, 1, n)


def _mod_slices(mod_ref, first):
    m = mod_ref[...]
    return [m[:, (first + k) * D_MODEL:(first + k + 1) * D_MODEL] for k in range(3)]


def _tile_specs(layer, tiles_per_batch, n_batch):
    x_spec = pl.BlockSpec((TM, D_MODEL), lambda t: (t, 0))
    mod_spec = pl.BlockSpec(
        (None, None, 1, N_MOD * D_MODEL),
        lambda t: (layer, jnp.minimum(t // tiles_per_batch, n_batch), 0, 0))
    g_spec = pl.BlockSpec((None, 4, D_MODEL), lambda t: (layer, 0, 0))
    return x_spec, mod_spec, g_spec


def _qkv_kernel(x_ref, mod_ref, g_ref, w_ref, tab_ref, q_ref, k_ref, vt_ref):
    sh, sc, _ = _mod_slices(mod_ref, 0)
    h = (_rms(x_ref[...], g_ref[0:1, :]) * (1.0 + sc) + sh).astype(BF16)
    cos = tab_ref[:, 0:HEAD_W]
    sin_up = tab_ref[:, HEAD_W:2 * HEAD_W]
    sin_dn = tab_ref[:, 2 * HEAD_W:3 * HEAD_W]
    for part, out_ref, scale in ((0, q_ref, Q_SCALE), (1, k_ref, None)):
        y = _dot(h, w_ref[:, part * D_MODEL:(part + 1) * D_MODEL])
        for hd in range(DA_HEADS):
            ys = y[:, hd * HEAD_W:(hd + 1) * HEAD_W]
            r = (ys * cos + pltpu.roll(ys, HEAD_W - ROPE_PAIRS, 1) * sin_up
                 + pltpu.roll(ys, ROPE_PAIRS, 1) * sin_dn)
            if scale is not None:
                r = r * scale
            out_ref[:, hd * HEAD_W:(hd + 1) * HEAD_W] = r.astype(BF16)
    v = _dot(h, w_ref[:, 2 * D_MODEL:3 * D_MODEL])
    for c in range(TM // TKV):
        vt_ref[c] = v[c * TKV:(c + 1) * TKV, :].T.astype(BF16)


def _qkv_project(x_all, mod, norm_g, w_qkv, rope_tab, layer, n_tiles, tiles_per_batch, n_batch):
    t_all = n_tiles * TM
    x_spec, mod_spec, g_spec = _tile_specs(layer, tiles_per_batch, n_batch)
    n_lat_tiles = tiles_per_batch * n_batch
    tab_spec = pl.BlockSpec(
        (TM, 3 * HEAD_W),
        lambda t: (jnp.where(t < n_lat_tiles, t % tiles_per_batch, tiles_per_batch), 0))
    return pl.pallas_call(
        _qkv_kernel,
        out_shape=(
            jax.ShapeDtypeStruct((t_all, D_MODEL), BF16),
            jax.ShapeDtypeStruct((t_all, D_MODEL), BF16),
            jax.ShapeDtypeStruct((t_all // TKV, D_MODEL, TKV), BF16),
        ),
        grid=(n_tiles,),
        in_specs=[x_spec, mod_spec, g_spec, _resident((D_MODEL, 3 * D_MODEL)), tab_spec],
        out_specs=(
            pl.BlockSpec((TM, D_MODEL), lambda t: (t, 0)),
            pl.BlockSpec((TM, D_MODEL), lambda t: (t, 0)),
            pl.BlockSpec((TM // TKV, D_MODEL, TKV), lambda t: (t, 0, 0)),
        ),
        compiler_params=pltpu.CompilerParams(
            dimension_semantics=("parallel",), vmem_limit_bytes=V7X_VMEM_LIMIT),
        name="qkv_project",
    )(x_all, mod, norm_g, w_qkv, rope_tab)


def _attn_kernel(*refs, n_chunks, lam_init):
    if n_chunks:
        lam_ref, g_ref, q_ref, kc_ref, vtc_ref, kl_ref, vtl_ref, o_ref = refs
    else:
        lam_ref, g_ref, q_ref, kc_ref, vtc_ref, o_ref = refs
    tq = q_ref.shape[0]
    qt = q_ref[...].astype(F32).T
    row = lax.broadcasted_iota(jnp.int32, qt.shape, 0)
    q2 = jnp.concatenate(
        [jnp.where(row < DA_HEAD_DIM, qt, 0.0), jnp.where(row >= DA_HEAD_DIM, qt, 0.0)],
        axis=1).astype(BF16)

    s = _dot(kc_ref[...], q2)
    m = jnp.max(s, axis=0, keepdims=True)
    p = jnp.exp2(s - m)
    l = jnp.sum(p, axis=0, keepdims=True)
    acc = _dot(vtc_ref[0], p.astype(BF16))

    if n_chunks:
        def body(i, carry):
            m, l, acc = carry
            off = pl.multiple_of(i * TKV, TKV)
            s = _dot(kl_ref[pl.ds(off, TKV), :], q2)
            m_new = jnp.maximum(m, jnp.max(s, axis=0, keepdims=True))
            alpha = jnp.exp2(m - m_new)
            p = jnp.exp2(s - m_new)
            l = alpha * l + jnp.sum(p, axis=0, keepdims=True)
            acc = alpha * acc + _dot(vtl_ref[i], p.astype(BF16))
            return m_new, l, acc
        m, l, acc = lax.fori_loop(0, n_chunks, body, (m, l, acc))

    lp = lam_ref[...]
    lam = (jnp.exp(jnp.sum(lp[0:1, :] * lp[1:2, :], axis=-1, keepdims=True))
           - jnp.exp(jnp.sum(lp[2:3, :] * lp[3:4, :], axis=-1, keepdims=True)) + lam_init)
    ot = acc[:, :tq] / l[:, :tq] - lam * (acc[:, tq:] / l[:, tq:])
    otn = ot * lax.rsqrt(jnp.mean(ot * ot, axis=0, keepdims=True) + EPS)
    o_ref[...] = (otn.T * g_ref[...] * (1.0 - lam_init)).astype(BF16)


def _attention(q, k, vt, lam_params, subln_g, lam_init, n_batch, seq, ctx_len, o_prev=None):
    t_lat = n_batch * seq
    ctx_q = o_prev is not None
    assert TKV % ctx_len == 0 and t_lat % TKV == 0 and seq % TKV == 0 and seq % TQ == 0
    per_chunk = TKV // ctx_len
    kc_spec = pl.BlockSpec((ctx_len, HEAD_W), lambda b, h, i: (t_lat // ctx_len + b, h))
    vtc_spec = pl.BlockSpec(
        (1, HEAD_W, ctx_len), lambda b, h, i: (t_lat // TKV + b // per_chunk, h, b % per_chunk))
    small = [pl.BlockSpec((4, DA_HEAD_DIM), lambda b, h, i: (0, 0)),
             pl.BlockSpec((1, HEAD_W), lambda b, h, i: (0, 0))]
    if ctx_q:
        tq, n_q, n_chunks = ctx_len, 1, 0
        q_map = lambda b, h, i: (t_lat // ctx_len + b, h)
        in_specs = small + [pl.BlockSpec((tq, HEAD_W), q_map), kc_spec, vtc_spec,
                            pl.BlockSpec(memory_space=pl.ANY)]
        args = (lam_params, subln_g, q, k, vt, o_prev)
        aliases = {5: 0}
        kernel = lambda *r, **kw: _attn_kernel(*r[:5], r[6], **kw)
    else:
        tq, n_q, n_chunks = TQ, seq // TQ, seq // TKV
        q_map = lambda b, h, i: (b * n_q + i, h)
        in_specs = small + [
            pl.BlockSpec((tq, HEAD_W), q_map), kc_spec, vtc_spec,
            pl.BlockSpec((seq, HEAD_W), lambda b, h, i: (b, h)),
            pl.BlockSpec((n_chunks, HEAD_W, TKV), lambda b, h, i: (b, h, 0))]
        args = (lam_params, subln_g, q, k, vt, k, vt)
        aliases = {}
        kernel = _attn_kernel
    return pl.pallas_call(
        functools.partial(kernel, n_chunks=n_chunks, lam_init=lam_init),
        out_shape=jax.ShapeDtypeStruct(q.shape, BF16),
        grid=(n_batch, DA_HEADS, n_q),
        in_specs=in_specs,
        out_specs=pl.BlockSpec((tq, HEAD_W), q_map),
        input_output_aliases=aliases,
        compiler_params=pltpu.CompilerParams(
            dimension_semantics=("parallel", "parallel", "arbitrary"),
            vmem_limit_bytes=V7X_VMEM_LIMIT),
        name="diff_attention_ctx" if ctx_q else "diff_attention",
    )(*args)


def _sgu_kernel(x_ref, mod_ref, g_ref, w_ref, b_ref, lng_ref, lnb_ref, ws_ref, bst_ref, o_ref):
    sh, sc, _ = _mod_slices(mod_ref, 0)
    h = (_rms(x_ref[...], g_ref[0:1, :]) * (1.0 + sc) + sh).astype(BF16)

    def gelu(z):
        return 0.5 * z * (1.0 + lax.erf(z * (2.0 ** -0.5)))

    zv = gelu(_dot(h, w_ref[:, SGU_HALF:]) + b_ref[:, SGU_HALF:])
    mu = jnp.mean(zv, axis=-1, keepdims=True)
    zc = zv - mu
    var = jnp.mean(zc * zc, axis=-1, keepdims=True)
    vn = (zc * lax.rsqrt(var + EPS) * lng_ref[...] + lnb_ref[...]).astype(BF16)
    zu = gelu(_dot(h, w_ref[:, :SGU_HALF]) + b_ref[:, :SGU_HALF])
    bst = bst_ref[...]
    for n in range(TM // CHUNK):
        rows = slice(n * CHUNK, (n + 1) * CHUNK)
        for g in range(SGU_GROUPS):
            cols = slice(g * SGU_GROUP_W, (g + 1) * SGU_GROUP_W)
            sv = _dot(ws_ref[g], vn[rows, cols]) + bst[:, g:g + 1]
            o_ref[rows, cols] = (zu[rows, cols] * sv).astype(BF16)


def _sgu_gate(x_all, mod, norm_g, w_in, b_in, ln_g, ln_b, w_s, b_s_t, layer, n_tiles,
              tiles_per_batch, n_batch):
    x_spec, mod_spec, g_spec = _tile_specs(layer, tiles_per_batch, n_batch)
    return pl.pallas_call(
        _sgu_kernel,
        out_shape=jax.ShapeDtypeStruct((n_tiles * TM, SGU_HALF), BF16),
        grid=(n_tiles,),
        in_specs=[x_spec, mod_spec, g_spec, _resident((D_MODEL, 2 * SGU_HALF)),
                  _resident((1, 2 * SGU_HALF)), _resident((1, SGU_HALF)), _resident((1, SGU_HALF)),
                  _resident((SGU_GROUPS, CHUNK, CHUNK)), _resident((CHUNK, SGU_GROUPS))],
        out_specs=pl.BlockSpec((TM, SGU_HALF), lambda t: (t, 0)),
        compiler_params=pltpu.CompilerParams(
            dimension_semantics=("parallel",), vmem_limit_bytes=V7X_VMEM_LIMIT),
        name="sgu_gate",
    )(x_all, mod, norm_g, w_in, b_in, ln_g, ln_b, w_s, b_s_t)


def _post_kernel(x_ref, u_ref, mod_ref, g_ref, wu_ref, w1_ref, w2_ref, o_ref):
    _, _, gt_a = _mod_slices(mod_ref, 0)
    sh_f, sc_f, gt_f = _mod_slices(mod_ref, 3)
    x1 = x_ref[...] + gt_a * _rms(_dot(u_ref[...], wu_ref[...]), g_ref[1:2, :])
    h = (_rms(x1, g_ref[2:3, :]) * (1.0 + sc_f) + sh_f).astype(BF16)
    f = jnp.zeros(x1.shape, F32)
    for c in range(MLP_HIDDEN // HID_TN):
        a = jnp.maximum(_dot(h, w1_ref[:, c * HID_TN:(c + 1) * HID_TN]), 0.0)
        f = f + _dot((a * a).astype(BF16), w2_ref[c * HID_TN:(c + 1) * HID_TN, :])
    o_ref[...] = x1 + gt_f * _rms(f, g_ref[3:4, :])


def _post_mixer(x_all, u, mod, norm_g, w_u, w1, w2, layer, n_tiles, tiles_per_batch, n_batch):
    x_spec, mod_spec, g_spec = _tile_specs(layer, tiles_per_batch, n_batch)
    ku = u.shape[1]
    return pl.pallas_call(
        _post_kernel,
        out_shape=jax.ShapeDtypeStruct((n_tiles * TM, D_MODEL), F32),
        grid=(n_tiles,),
        in_specs=[x_spec, pl.BlockSpec((TM, ku), lambda t: (t, 0)), mod_spec, g_spec,
                  _resident((ku, D_MODEL)), _resident((D_MODEL, MLP_HIDDEN)),
                  _resident((MLP_HIDDEN, D_MODEL))],
        out_specs=pl.BlockSpec((TM, D_MODEL), lambda t: (t, 0)),
        compiler_params=pltpu.CompilerParams(
            dimension_semantics=("parallel",), vmem_limit_bytes=V7X_VMEM_LIMIT),
        name="post_mixer_mlp",
    )(x_all, u, mod, norm_g, w_u, w1, w2)


def _rope_table(seq):
    n_rows = seq // GRID_W
    rows = jnp.broadcast_to(jnp.arange(n_rows, dtype=F32)[:, None], (n_rows, GRID_W)).reshape(-1)
    cols = jnp.broadcast_to(jnp.arange(GRID_W, dtype=F32)[None, :], (n_rows, GRID_W)).reshape(-1)
    inv_freq = ROPE_THETA ** (-jnp.arange(ROPE_PAIRS, dtype=F32) / ROPE_PAIRS)
    ang_r = rows[:, None] * inv_freq
    ang_c = cols[:, None] * inv_freq
    ang = jnp.concatenate([ang_r, ang_r, ang_c, ang_c] * 2, axis=-1)
    cos, sin = jnp.cos(ang), jnp.sin(ang)
    low_half = (jnp.arange(HEAD_W) % (2 * ROPE_PAIRS)) < ROPE_PAIRS
    tab = jnp.concatenate(
        [cos, jnp.where(low_half, -sin, 0.0), jnp.where(low_half, 0.0, sin)], axis=-1)
    ident = jnp.concatenate(
        [jnp.ones((TM, HEAD_W), F32), jnp.zeros((TM, 2 * HEAD_W), F32)], axis=-1)
    return jnp.concatenate([tab, ident], axis=0)


def kernel(x, c, ctx, c_ctx, w_mod, b_mod, norm_g, da_w_qkv, da_w_o, da_lambda, da_subln_g,
           sgu_w_in, sgu_b_in, sgu_ln_g, sgu_ln_b, sgu_w_s, sgu_b_s, sgu_w_out, mlp_w1, mlp_w2):
    n_batch, seq, d = x.shape
    ctx_len = ctx.shape[1]
    assert d == D_MODEL and seq % TM == 0 and (n_batch * ctx_len) % TM == 0
    assert n_batch < MOD_ROWS and TM % TKV == 0 and TM % CHUNK == 0 and ctx_len % CHUNK == 0
    tiles_per_batch = seq // TM
    lat_tiles = n_batch * tiles_per_batch
    all_tiles = lat_tiles + n_batch * ctx_len // TM

    cvec = jnp.concatenate(
        [c, c_ctx[None, :], jnp.zeros((MOD_ROWS - n_batch - 1, d), F32)], axis=0)
    mod = _modulation(cvec, w_mod, b_mod)
    rope_tab = _rope_table(seq)
    x_all = jnp.concatenate([x.reshape(n_batch * seq, d), ctx.reshape(n_batch * ctx_len, d)], axis=0)

    for i in range(DEPTH):
        j = i // 2
        later_attn = any(l % 2 == 0 for l in range(i + 1, DEPTH))
        w1 = mlp_w1[i].astype(BF16)
        w2 = mlp_w2[i].astype(BF16)
        if i % 2 == 0:
            lam_init = 0.8 - 0.6 * math.exp(-0.3 * i)
            g_head = da_subln_g[j].reshape(1, HEAD_W)
            q, k, vt = _qkv_project(x_all, mod, norm_g, da_w_qkv[j].astype(BF16), rope_tab, i,
                                    all_tiles, tiles_per_batch, n_batch)
            u = _attention(q, k, vt, da_lambda[j], g_head, lam_init, n_batch, seq, ctx_len)
            if later_attn:
                u = _attention(q, k, vt, da_lambda[j], g_head, lam_init, n_batch, seq, ctx_len,
                               o_prev=u)
            w_u = da_w_o[j].astype(BF16)
            n_tiles = all_tiles if later_attn else lat_tiles
        else:
            n_tiles = all_tiles if later_attn else lat_tiles
            u = _sgu_gate(x_all, mod, norm_g, sgu_w_in[j].astype(BF16),
                          sgu_b_in[j].reshape(1, -1), sgu_ln_g[j].reshape(1, -1),
                          sgu_ln_b[j].reshape(1, -1), sgu_w_s[j].astype(BF16), sgu_b_s[j].T,
                          i, n_tiles, tiles_per_batch, n_batch)
            w_u = sgu_w_out[j].astype(BF16)
        x_all = _post_mixer(x_all, u, mod, norm_g, w_u, w1, w2, i, n_tiles, tiles_per_batch,
                            n_batch)
    return x_all[:n_batch * seq].reshape(n_batch, seq, d)
```

```python
import functools
import math

import jax
import jax.numpy as jnp
from jax import lax
from jax.experimental import pallas as pl
from jax.experimental.pallas import tpu as pltpu

D_MODEL = 1024
DEPTH = 4
GRID_W = 64
N_MOD = 6
EPS = 1e-6
DA_HEADS = 8
DA_HEAD_DIM = 64
HEAD_W = 2 * DA_HEAD_DIM
ROPE_THETA = 10000.0
ROPE_PAIRS = 16
SGU_HALF = 2048
SGU_GROUPS = 8
SGU_GROUP_W = SGU_HALF // SGU_GROUPS
CHUNK = 128
MLP_HIDDEN = 4 * D_MODEL

TM = 512
TQ = 512
ATTN_UNROLL = 4
TKV = 512
MOD_ROWS = 8
MOD_TN = 1536
HID_TN = 1024
V7X_VMEM_LIMIT = 56 * 1024 * 1024
Q_SCALE = DA_HEAD_DIM ** -0.5 * math.log2(math.e)

BF16 = jnp.bfloat16
F32 = jnp.float32


def _dot(a, b):
    return jnp.dot(a, b, preferred_element_type=F32)


def _rms(x, g):
    return x * lax.rsqrt(jnp.mean(x * x, axis=-1, keepdims=True) + EPS) * g


def _resident(shape):
    return pl.BlockSpec(shape, lambda *_: (0,) * len(shape), pipeline_mode=pl.Buffered(1))


def _mod_kernel(cv_ref, w_ref, b_ref, o_ref):
    cv = cv_ref[...]
    s = cv / (1.0 + jnp.exp(-cv))
    o_ref[...] = _dot(s.astype(BF16), w_ref[...].astype(BF16)) + b_ref[...]


def _modulation(cvec, w_mod, b_mod):
    n = N_MOD * D_MODEL
    out = pl.pallas_call(
        _mod_kernel,
        out_shape=jax.ShapeDtypeStruct((DEPTH, MOD_ROWS, n), F32),
        grid=(DEPTH, n // MOD_TN),
        in_specs=[
            pl.BlockSpec((MOD_ROWS, D_MODEL), lambda i, j: (0, 0)),
            pl.BlockSpec((None, D_MODEL, MOD_TN), lambda i, j: (i, 0, j)),
            pl.BlockSpec((None, 1, MOD_TN), lambda i, j: (i, 0, j)),
        ],
        out_specs=pl.BlockSpec((None, MOD_ROWS, MOD_TN), lambda i, j: (i, 0, j)),
        compiler_params=pltpu.CompilerParams(
            dimension_semantics=("parallel", "parallel"), vmem_limit_bytes=V7X_VMEM_LIMIT),
        name="modulation",
    )(cvec, w_mod, b_mod.reshape(DEPTH, 1, n))
    return out.reshape(DEPTH, MOD_ROWS, 1, n)


def _mod_slices(mod_ref, first):
    m = mod_ref[...]
    return [m[:, (first + k) * D_MODEL:(first + k + 1) * D_MODEL] for k in range(3)]


def _tile_specs(layer, tiles_per_batch, n_batch):
    x_spec = pl.BlockSpec((TM, D_MODEL), lambda t: (t, 0))
    mod_spec = pl.BlockSpec(
        (None, None, 1, N_MOD * D_MODEL),
        lambda t: (layer, jnp.minimum(t // tiles_per_batch, n_batch), 0, 0))
    g_spec = pl.BlockSpec((None, 4, D_MODEL), lambda t: (layer, 0, 0))
    return x_spec, mod_spec, g_spec


def _qkv_kernel(x_ref, mod_ref, g_ref, w_ref, tab_ref, q_ref, k_ref, vt_ref):
    sh, sc, _ = _mod_slices(mod_ref, 0)
    h = (_rms(x_ref[...], g_ref[0:1, :]) * (1.0 + sc) + sh).astype(BF16)
    cos = tab_ref[:, 0:HEAD_W]
    sin_up = tab_ref[:, HEAD_W:2 * HEAD_W]
    sin_dn = tab_ref[:, 2 * HEAD_W:3 * HEAD_W]
    for part, out_ref, scale in ((0, q_ref, Q_SCALE), (1, k_ref, None)):
        y = _dot(h, w_ref[:, part * D_MODEL:(part + 1) * D_MODEL])
        for hd in range(DA_HEADS):
            ys = y[:, hd * HEAD_W:(hd + 1) * HEAD_W]
            r = (ys * cos + pltpu.roll(ys, HEAD_W - ROPE_PAIRS, 1) * sin_up
                 + pltpu.roll(ys, ROPE_PAIRS, 1) * sin_dn)
            if scale is not None:
                r = r * scale
            out_ref[:, hd * HEAD_W:(hd + 1) * HEAD_W] = r.astype(BF16)
    v = _dot(h, w_ref[:, 2 * D_MODEL:3 * D_MODEL])
    for c in range(TM // TKV):
        vt_ref[c] = v[c * TKV:(c + 1) * TKV, :].T.astype(BF16)


def _qkv_project(x_all, mod, norm_g, w_qkv, rope_tab, layer, n_tiles, tiles_per_batch, n_batch):
    t_all = n_tiles * TM
    x_spec, mod_spec, g_spec = _tile_specs(layer, tiles_per_batch, n_batch)
    n_lat_tiles = tiles_per_batch * n_batch
    tab_spec = pl.BlockSpec(
        (TM, 3 * HEAD_W),
        lambda t: (jnp.where(t < n_lat_tiles, t % tiles_per_batch, tiles_per_batch), 0))
    return pl.pallas_call(
        _qkv_kernel,
        out_shape=(
            jax.ShapeDtypeStruct((t_all, D_MODEL), BF16),
            jax.ShapeDtypeStruct((t_all, D_MODEL), BF16),
            jax.ShapeDtypeStruct((t_all // TKV, D_MODEL, TKV), BF16),
        ),
        grid=(n_tiles,),
        in_specs=[x_spec, mod_spec, g_spec, _resident((D_MODEL, 3 * D_MODEL)), tab_spec],
        out_specs=(
            pl.BlockSpec((TM, D_MODEL), lambda t: (t, 0)),
            pl.BlockSpec((TM, D_MODEL), lambda t: (t, 0)),
            pl.BlockSpec((TM // TKV, D_MODEL, TKV), lambda t: (t, 0, 0)),
        ),
        compiler_params=pltpu.CompilerParams(
            dimension_semantics=("parallel",), vmem_limit_bytes=V7X_VMEM_LIMIT),
        name="qkv_project",
    )(x_all, mod, norm_g, w_qkv, rope_tab)


def _stack_query_maps(q_ref):
    qt = q_ref[...].astype(F32).T
    row = lax.broadcasted_iota(jnp.int32, qt.shape, 0)
    return jnp.concatenate(
        [jnp.where(row < DA_HEAD_DIM, qt, 0.0), jnp.where(row >= DA_HEAD_DIM, qt, 0.0)],
        axis=1).astype(BF16)


def _first_chunk(k, vt, q2):
    s = _dot(k, q2)
    m = jnp.max(s, axis=0, keepdims=True)
    p = jnp.exp2(s - m)
    return m, jnp.sum(p, axis=0, keepdims=True), _dot(vt, p.astype(BF16))


def _finish_heads(acc, l, lam_ref, g_ref, o_ref, lam_init):
    tq = o_ref.shape[0]
    lp = lam_ref[...]
    lam = (jnp.exp(jnp.sum(lp[0:1, :] * lp[1:2, :], axis=-1, keepdims=True))
           - jnp.exp(jnp.sum(lp[2:3, :] * lp[3:4, :], axis=-1, keepdims=True)) + lam_init)
    ot = acc[:, :tq] / l[:, :tq] - lam * (acc[:, tq:] / l[:, tq:])
    otn = ot * lax.rsqrt(jnp.mean(ot * ot, axis=0, keepdims=True) + EPS)
    o_ref[...] = (otn.T * g_ref[...] * (1.0 - lam_init)).astype(BF16)


def _attn_ctx_kernel(lam_ref, g_ref, q_ref, kc_ref, vtc_ref, o_prev_ref, o_ref, *, lam_init):
    del o_prev_ref
    _, l, acc = _first_chunk(kc_ref[...], vtc_ref[0], _stack_query_maps(q_ref))
    _finish_heads(acc, l, lam_ref, g_ref, o_ref, lam_init)


def _attn_kernel(lam_ref, g_ref, q_ref, kc_ref, vtc_ref, kl_ref, vtl_ref, o_ref,
                 q2_ref, s_ref, acc_ref, *, n_chunks, lam_init):
    q2_ref[...] = _stack_query_maps(q_ref)

    def scores(slot, chunk):
        off = chunk * TKV if isinstance(chunk, int) else pl.multiple_of(chunk * TKV, TKV)
        s = _dot(kl_ref[pl.ds(off, TKV), :], q2_ref[...])
        s_ref[slot] = s
        return jnp.max(s, axis=0, keepdims=True)

    def absorb(slot, chunk, mx, m, l):
        m_new = jnp.maximum(m, mx)
        alpha = jnp.exp2(m - m_new)
        p = jnp.exp2(s_ref[slot] - m_new)
        acc_ref[...] = alpha * acc_ref[...] + _dot(vtl_ref[chunk], p.astype(BF16))
        return m_new, alpha * l + jnp.sum(p, axis=0, keepdims=True)

    m, l, acc = _first_chunk(kc_ref[...], vtc_ref[0], q2_ref[...])
    acc_ref[...] = acc
    mx = scores(0, 0)

    def run(first, count, last, m, l, mx):
        for u in range(count):
            mx_next = None if last and u == count - 1 else scores((u + 1) % 2, first + u + 1)
            m, l = absorb(u % 2, first + u, mx, m, l)
            mx = mx_next
        return m, l, mx

    trips = (n_chunks - 1) // ATTN_UNROLL
    m, l, mx = lax.fori_loop(
        0, trips, lambda j, c: run(j * ATTN_UNROLL, ATTN_UNROLL, False, *c), (m, l, mx))
    done = trips * ATTN_UNROLL
    m, l, _ = run(done, n_chunks - done, True, m, l, mx)
    _finish_heads(acc_ref[...], l, lam_ref, g_ref, o_ref, lam_init)


def _attention(q, k, vt, lam_params, subln_g, lam_init, n_batch, seq, ctx_len, o_prev=None):
    t_lat = n_batch * seq
    ctx_q = o_prev is not None
    assert TKV % ctx_len == 0 and t_lat % TKV == 0 and seq % (2 * TKV) == 0 and seq % TQ == 0
    per_chunk = TKV // ctx_len
    kc_spec = pl.BlockSpec((ctx_len, HEAD_W), lambda b, h, i: (t_lat // ctx_len + b, h))
    vtc_spec = pl.BlockSpec(
        (1, HEAD_W, ctx_len), lambda b, h, i: (t_lat // TKV + b // per_chunk, h, b % per_chunk))
    small = [pl.BlockSpec((4, DA_HEAD_DIM), lambda b, h, i: (0, 0)),
             pl.BlockSpec((1, HEAD_W), lambda b, h, i: (0, 0))]
    if ctx_q:
        tq, n_q = ctx_len, 1
        q_map = lambda b, h, i: (t_lat // ctx_len + b, h)
        in_specs = small + [pl.BlockSpec((tq, HEAD_W), q_map), kc_spec, vtc_spec,
                            pl.BlockSpec(memory_space=pl.ANY)]
        args = (lam_params, subln_g, q, k, vt, o_prev)
        aliases = {5: 0}
        kernel = functools.partial(_attn_ctx_kernel, lam_init=lam_init)
        scratch = []
    else:
        tq, n_q, n_chunks = TQ, seq // TQ, seq // TKV
        q_map = lambda b, h, i: (b * n_q + i, h)
        in_specs = small + [
            pl.BlockSpec((tq, HEAD_W), q_map), kc_spec, vtc_spec,
            pl.BlockSpec((seq, HEAD_W), lambda b, h, i: (b, h)),
            pl.BlockSpec((n_chunks, HEAD_W, TKV), lambda b, h, i: (b, h, 0))]
        args = (lam_params, subln_g, q, k, vt, k, vt)
        aliases = {}
        kernel = functools.partial(_attn_kernel, n_chunks=n_chunks, lam_init=lam_init)
        scratch = [pltpu.VMEM((HEAD_W, 2 * tq), BF16), pltpu.VMEM((2, TKV, 2 * tq), F32),
                   pltpu.VMEM((HEAD_W, 2 * tq), F32)]
    return pl.pallas_call(
        kernel,
        out_shape=jax.ShapeDtypeStruct(q.shape, BF16),
        grid=(n_batch, DA_HEADS, n_q),
        in_specs=in_specs,
        out_specs=pl.BlockSpec((tq, HEAD_W), q_map),
        scratch_shapes=scratch,
        input_output_aliases=aliases,
        compiler_params=pltpu.CompilerParams(
            dimension_semantics=("parallel", "parallel", "arbitrary"),
            vmem_limit_bytes=V7X_VMEM_LIMIT),
        name="diff_attention_ctx" if ctx_q else "diff_attention",
    )(*args)


def _sgu_kernel(x_ref, mod_ref, g_ref, w_ref, b_ref, lng_ref, lnb_ref, ws_ref, bst_ref, o_ref):
    sh, sc, _ = _mod_slices(mod_ref, 0)
    h = (_rms(x_ref[...], g_ref[0:1, :]) * (1.0 + sc) + sh).astype(BF16)

    def gelu(z):
        return 0.5 * z * (1.0 + lax.erf(z * (2.0 ** -0.5)))

    zv = gelu(_dot(h, w_ref[:, SGU_HALF:]) + b_ref[:, SGU_HALF:])
    mu = jnp.mean(zv, axis=-1, keepdims=True)
    zc = zv - mu
    var = jnp.mean(zc * zc, axis=-1, keepdims=True)
    vn = (zc * lax.rsqrt(var + EPS) * lng_ref[...] + lnb_ref[...]).astype(BF16)
    zu = gelu(_dot(h, w_ref[:, :SGU_HALF]) + b_ref[:, :SGU_HALF])
    bst = bst_ref[...]
    for n in range(TM // CHUNK):
        rows = slice(n * CHUNK, (n + 1) * CHUNK)
        for g in range(SGU_GROUPS):
            cols = slice(g * SGU_GROUP_W, (g + 1) * SGU_GROUP_W)
            sv = _dot(ws_ref[g], vn[rows, cols]) + bst[:, g:g + 1]
            o_ref[rows, cols] = (zu[rows, cols] * sv).astype(BF16)


def _sgu_gate(x_all, mod, norm_g, w_in, b_in, ln_g, ln_b, w_s, b_s_t, layer, n_tiles,
              tiles_per_batch, n_batch):
    x_spec, mod_spec, g_spec = _tile_specs(layer, tiles_per_batch, n_batch)
    return pl.pallas_call(
        _sgu_kernel,
        out_shape=jax.ShapeDtypeStruct((n_tiles * TM, SGU_HALF), BF16),
        grid=(n_tiles,),
        in_specs=[x_spec, mod_spec, g_spec, _resident((D_MODEL, 2 * SGU_HALF)),
                  _resident((1, 2 * SGU_HALF)), _resident((1, SGU_HALF)), _resident((1, SGU_HALF)),
                  _resident((SGU_GROUPS, CHUNK, CHUNK)), _resident((CHUNK, SGU_GROUPS))],
        out_specs=pl.BlockSpec((TM, SGU_HALF), lambda t: (t, 0)),
        compiler_params=pltpu.CompilerParams(
            dimension_semantics=("parallel",), vmem_limit_bytes=V7X_VMEM_LIMIT),
        name="sgu_gate",
    )(x_all, mod, norm_g, w_in, b_in, ln_g, ln_b, w_s, b_s_t)


def _post_kernel(x_ref, u_ref, mod_ref, g_ref, wu_ref, w1_ref, w2_ref, o_ref):
    _, _, gt_a = _mod_slices(mod_ref, 0)
    sh_f, sc_f, gt_f = _mod_slices(mod_ref, 3)
    x1 = x_ref[...] + gt_a * _rms(_dot(u_ref[...], wu_ref[...]), g_ref[1:2, :])
    h = (_rms(x1, g_ref[2:3, :]) * (1.0 + sc_f) + sh_f).astype(BF16)
    f = jnp.zeros(x1.shape, F32)
    for c in range(MLP_HIDDEN // HID_TN):
        a = jnp.maximum(_dot(h, w1_ref[:, c * HID_TN:(c + 1) * HID_TN]), 0.0)
        f = f + _dot((a * a).astype(BF16), w2_ref[c * HID_TN:(c + 1) * HID_TN, :])
    o_ref[...] = x1 + gt_f * _rms(f, g_ref[3:4, :])


def _post_mixer(x_all, u, mod, norm_g, w_u, w1, w2, layer, n_tiles, tiles_per_batch, n_batch):
    x_spec, mod_spec, g_spec = _tile_specs(layer, tiles_per_batch, n_batch)
    ku = u.shape[1]
    return pl.pallas_call(
        _post_kernel,
        out_shape=jax.ShapeDtypeStruct((n_tiles * TM, D_MODEL), F32),
        grid=(n_tiles,),
        in_specs=[x_spec, pl.BlockSpec((TM, ku), lambda t: (t, 0)), mod_spec, g_spec,
                  _resident((ku, D_MODEL)), _resident((D_MODEL, MLP_HIDDEN)),
                  _resident((MLP_HIDDEN, D_MODEL))],
        out_specs=pl.BlockSpec((TM, D_MODEL), lambda t: (t, 0)),
        compiler_params=pltpu.CompilerParams(
            dimension_semantics=("parallel",), vmem_limit_bytes=V7X_VMEM_LIMIT),
        name="post_mixer_mlp",
    )(x_all, u, mod, norm_g, w_u, w1, w2)


def _rope_table(seq):
    n_rows = seq // GRID_W
    rows = jnp.broadcast_to(jnp.arange(n_rows, dtype=F32)[:, None], (n_rows, GRID_W)).reshape(-1)
    cols = jnp.broadcast_to(jnp.arange(GRID_W, dtype=F32)[None, :], (n_rows, GRID_W)).reshape(-1)
    inv_freq = ROPE_THETA ** (-jnp.arange(ROPE_PAIRS, dtype=F32) / ROPE_PAIRS)
    ang_r = rows[:, None] * inv_freq
    ang_c = cols[:, None] * inv_freq
    ang = jnp.concatenate([ang_r, ang_r, ang_c, ang_c] * 2, axis=-1)
    cos, sin = jnp.cos(ang), jnp.sin(ang)
    low_half = (jnp.arange(HEAD_W) % (2 * ROPE_PAIRS)) < ROPE_PAIRS
    tab = jnp.concatenate(
        [cos, jnp.where(low_half, -sin, 0.0), jnp.where(low_half, 0.0, sin)], axis=-1)
    ident = jnp.concatenate(
        [jnp.ones((TM, HEAD_W), F32), jnp.zeros((TM, 2 * HEAD_W), F32)], axis=-1)
    return jnp.concatenate([tab, ident], axis=0)


def kernel(x, c, ctx, c_ctx, w_mod, b_mod, norm_g, da_w_qkv, da_w_o, da_lambda, da_subln_g,
           sgu_w_in, sgu_b_in, sgu_ln_g, sgu_ln_b, sgu_w_s, sgu_b_s, sgu_w_out, mlp_w1, mlp_w2):
    n_batch, seq, d = x.shape
    ctx_len = ctx.shape[1]
    assert d == D_MODEL and seq % TM == 0 and (n_batch * ctx_len) % TM == 0
    assert n_batch < MOD_ROWS and TM % TKV == 0 and TM % CHUNK == 0 and ctx_len % CHUNK == 0
    tiles_per_batch = seq // TM
    lat_tiles = n_batch * tiles_per_batch
    all_tiles = lat_tiles + n_batch * ctx_len // TM

    cvec = jnp.concatenate(
        [c, c_ctx[None, :], jnp.zeros((MOD_ROWS - n_batch - 1, d), F32)], axis=0)
    mod = _modulation(cvec, w_mod, b_mod)
    rope_tab = _rope_table(seq)
    x_all = jnp.concatenate([x.reshape(n_batch * seq, d), ctx.reshape(n_batch * ctx_len, d)], axis=0)

    for i in range(DEPTH):
        j = i // 2
        later_attn = any(l % 2 == 0 for l in range(i + 1, DEPTH))
        w1 = mlp_w1[i].astype(BF16)
        w2 = mlp_w2[i].astype(BF16)
        if i % 2 == 0:
            lam_init = 0.8 - 0.6 * math.exp(-0.3 * i)
            g_head = da_subln_g[j].reshape(1, HEAD_W)
            q, k, vt = _qkv_project(x_all, mod, norm_g, da_w_qkv[j].astype(BF16), rope_tab, i,
                                    all_tiles, tiles_per_batch, n_batch)
            u = _attention(q, k, vt, da_lambda[j], g_head, lam_init, n_batch, seq, ctx_len)
            if later_attn:
                u = _attention(q, k, vt, da_lambda[j], g_head, lam_init, n_batch, seq, ctx_len,
                               o_prev=u)
            w_u = da_w_o[j].astype(BF16)
            n_tiles = all_tiles if later_attn else lat_tiles
        else:
            n_tiles = all_tiles if later_attn else lat_tiles
            u = _sgu_gate(x_all, mod, norm_g, sgu_w_in[j].astype(BF16),
                          sgu_b_in[j].reshape(1, -1), sgu_ln_g[j].reshape(1, -1),
                          sgu_ln_b[j].reshape(1, -1), sgu_w_s[j].astype(BF16), sgu_b_s[j].T,
                          i, n_tiles, tiles_per_batch, n_batch)
            w_u = sgu_w_out[j].astype(BF16)
        x_all = _post_mixer(x_all, u, mod, norm_g, w_u, w1, w2, i, n_tiles, tiles_per_batch,
                            n_batch)
    return x_all[:n_batch * seq].reshape(n_batch, seq, d)
```

```python
import functools
import math

import jax
import jax.numpy as jnp
from jax import lax
from jax.experimental import pallas as pl
from jax.experimental.pallas import tpu as pltpu

D_MODEL = 1024
DEPTH = 4
GRID_W = 64
N_MOD = 6
EPS = 1e-6
DA_HEADS = 8
DA_HEAD_DIM = 64
HEAD_W = 2 * DA_HEAD_DIM
ROPE_THETA = 10000.0
ROPE_PAIRS = 16
SGU_HALF = 2048
SGU_GROUPS = 8
SGU_GROUP_W = SGU_HALF // SGU_GROUPS
CHUNK = 128
MLP_HIDDEN = 4 * D_MODEL

TM = 512
TQ = 1024
ATTN_UNROLL = 4
TKV = 512
MOD_ROWS = 8
MOD_TN = 1536
HID_TN = 1024
POST_ROWS = 256
V7X_VMEM_LIMIT = 56 * 1024 * 1024
Q_SCALE = DA_HEAD_DIM ** -0.5 * math.log2(math.e)
SUM_MIN, SUM_MAX, ACC_MAX = 2.0 ** -60, 2.0 ** 60, 2.0 ** 120

BF16 = jnp.bfloat16
F32 = jnp.float32


def _dot(a, b):
    return jnp.dot(a, b, preferred_element_type=F32)


def _rms(x, g):
    return x * lax.rsqrt(jnp.mean(x * x, axis=-1, keepdims=True) + EPS) * g


def _resident(shape):
    return pl.BlockSpec(shape, lambda *_: (0,) * len(shape), pipeline_mode=pl.Buffered(1))


def _mod_kernel(cv_ref, w_ref, b_ref, o_ref):
    cv = cv_ref[...]
    s = cv / (1.0 + jnp.exp(-cv))
    o_ref[...] = _dot(s.astype(BF16), w_ref[...].astype(BF16)) + b_ref[...]


def _modulation(cvec, w_mod, b_mod):
    n = N_MOD * D_MODEL
    out = pl.pallas_call(
        _mod_kernel,
        out_shape=jax.ShapeDtypeStruct((DEPTH, MOD_ROWS, n), F32),
        grid=(DEPTH, n // MOD_TN),
        in_specs=[
            pl.BlockSpec((MOD_ROWS, D_MODEL), lambda i, j: (0, 0)),
            pl.BlockSpec((None, D_MODEL, MOD_TN), lambda i, j: (i, 0, j)),
            pl.BlockSpec((None, 1, MOD_TN), lambda i, j: (i, 0, j)),
        ],
        out_specs=pl.BlockSpec((None, MOD_ROWS, MOD_TN), lambda i, j: (i, 0, j)),
        compiler_params=pltpu.CompilerParams(
            dimension_semantics=("parallel", "parallel"), vmem_limit_bytes=V7X_VMEM_LIMIT),
        name="modulation",
    )(cvec, w_mod, b_mod.reshape(DEPTH, 1, n))
    return out.reshape(DEPTH, MOD_ROWS, 1, n)


def _mod_slices(mod_ref, first):
    m = mod_ref[...]
    return [m[:, (first + k) * D_MODEL:(first + k + 1) * D_MODEL] for k in range(3)]


def _tile_specs(layer, tiles_per_batch, n_batch):
    x_spec = pl.BlockSpec((TM, D_MODEL), lambda t: (t, 0))
    mod_spec = pl.BlockSpec(
        (None, None, 1, N_MOD * D_MODEL),
        lambda t: (layer, jnp.minimum(t // tiles_per_batch, n_batch), 0, 0))
    g_spec = pl.BlockSpec((None, 4, D_MODEL), lambda t: (layer, 0, 0))
    return x_spec, mod_spec, g_spec


def _qkv_kernel(x_ref, mod_ref, g_ref, w_ref, tab_ref, q_ref, k_ref, vt_ref):
    sh, sc, _ = _mod_slices(mod_ref, 0)
    h = (_rms(x_ref[...], g_ref[0:1, :]) * (1.0 + sc) + sh).astype(BF16)
    cos = tab_ref[:, 0:HEAD_W]
    sin_up = tab_ref[:, HEAD_W:2 * HEAD_W]
    sin_dn = tab_ref[:, 2 * HEAD_W:3 * HEAD_W]
    for part, out_ref, scale in ((0, q_ref, Q_SCALE), (1, k_ref, None)):
        y = _dot(h, w_ref[:, part * D_MODEL:(part + 1) * D_MODEL])
        for hd in range(DA_HEADS):
            ys = y[:, hd * HEAD_W:(hd + 1) * HEAD_W]
            r = (ys * cos + pltpu.roll(ys, HEAD_W - ROPE_PAIRS, 1) * sin_up
                 + pltpu.roll(ys, ROPE_PAIRS, 1) * sin_dn)
            if scale is not None:
                r = r * scale
            out_ref[:, hd * HEAD_W:(hd + 1) * HEAD_W] = r.astype(BF16)
    v = _dot(h, w_ref[:, 2 * D_MODEL:3 * D_MODEL])
    for c in range(TM // TKV):
        vt_ref[c] = v[c * TKV:(c + 1) * TKV, :].T.astype(BF16)


def _qkv_project(x_all, mod, norm_g, w_qkv, rope_tab, layer, n_tiles, tiles_per_batch, n_batch):
    t_all = n_tiles * TM
    x_spec, mod_spec, g_spec = _tile_specs(layer, tiles_per_batch, n_batch)
    n_lat_tiles = tiles_per_batch * n_batch
    tab_spec = pl.BlockSpec(
        (TM, 3 * HEAD_W),
        lambda t: (jnp.where(t < n_lat_tiles, t % tiles_per_batch, tiles_per_batch), 0))
    return pl.pallas_call(
        _qkv_kernel,
        out_shape=(
            jax.ShapeDtypeStruct((t_all, D_MODEL), BF16),
            jax.ShapeDtypeStruct((t_all, D_MODEL), BF16),
            jax.ShapeDtypeStruct((t_all // TKV, D_MODEL, TKV), BF16),
        ),
        grid=(n_tiles,),
        in_specs=[x_spec, mod_spec, g_spec, _resident((D_MODEL, 3 * D_MODEL)), tab_spec],
        out_specs=(
            pl.BlockSpec((TM, D_MODEL), lambda t: (t, 0)),
            pl.BlockSpec((TM, D_MODEL), lambda t: (t, 0)),
            pl.BlockSpec((TM // TKV, D_MODEL, TKV), lambda t: (t, 0, 0)),
        ),
        compiler_params=pltpu.CompilerParams(
            dimension_semantics=("parallel",), vmem_limit_bytes=V7X_VMEM_LIMIT),
        name="qkv_project",
    )(x_all, mod, norm_g, w_qkv, rope_tab)


def _stack_query_maps(q_ref):
    qt = q_ref[...].astype(F32).T
    row = lax.broadcasted_iota(jnp.int32, qt.shape, 0)
    return jnp.concatenate(
        [jnp.where(row < DA_HEAD_DIM, qt, 0.0), jnp.where(row >= DA_HEAD_DIM, qt, 0.0)],
        axis=1).astype(BF16)


def _first_chunk(k, vt, q2):
    s = _dot(k, q2)
    m = jnp.max(s, axis=0, keepdims=True)
    p = jnp.exp2(s - m)
    return m, jnp.sum(p, axis=0, keepdims=True), _dot(vt, p.astype(BF16))


def _finish_heads(acc, l, lam_ref, g_ref, o_ref, lam_init):
    tq = o_ref.shape[0]
    lp = lam_ref[...]
    lam = (jnp.exp(jnp.sum(lp[0:1, :] * lp[1:2, :], axis=-1, keepdims=True))
           - jnp.exp(jnp.sum(lp[2:3, :] * lp[3:4, :], axis=-1, keepdims=True)) + lam_init)
    ot = acc[:, :tq] / l[:, :tq] - lam * (acc[:, tq:] / l[:, tq:])
    otn = ot * lax.rsqrt(jnp.mean(ot * ot, axis=0, keepdims=True) + EPS)
    o_ref[...] = (otn.T * g_ref[...] * (1.0 - lam_init)).astype(BF16)


def _attn_ctx_kernel(lam_ref, g_ref, q_ref, kc_ref, vtc_ref, o_prev_ref, o_ref, *, lam_init):
    del o_prev_ref
    _, l, acc = _first_chunk(kc_ref[...], vtc_ref[0], _stack_query_maps(q_ref))
    _finish_heads(acc, l, lam_ref, g_ref, o_ref, lam_init)


def _chunk_pipeline(n_chunks, produce, consume, carry):
    def run(first, count, last, carry, out):
        for u in range(count):
            nxt = None if last and u == count - 1 else produce((u + 1) % 2, first + u + 1)
            carry = consume(u % 2, first + u, out, carry)
            out = nxt
        return carry, out

    trips = (n_chunks - 1) // ATTN_UNROLL
    carry, out = lax.fori_loop(
        0, trips, lambda j, c: run(j * ATTN_UNROLL, ATTN_UNROLL, False, *c),
        (carry, produce(0, 0)))
    done = trips * ATTN_UNROLL
    return run(done, n_chunks - done, True, carry, out)[0]


def _attn_kernel(lam_ref, g_ref, q_ref, kc_ref, vtc_ref, kl_ref, vtl_ref, o_ref,
                 q2_ref, p_ref, acc_ref, *, n_chunks, lam_init):
    q2_ref[...] = _stack_query_maps(q_ref)

    def key_rows(chunk):
        off = chunk * TKV if isinstance(chunk, int) else pl.multiple_of(chunk * TKV, TKV)
        return kl_ref[pl.ds(off, TKV), :]

    def produce_unshifted(slot, chunk):
        p = jnp.exp2(_dot(key_rows(chunk), q2_ref[...]))
        p_ref[slot] = p.astype(BF16)
        return jnp.sum(p, axis=0, keepdims=True)

    def consume_unshifted(slot, chunk, p_sum, l):
        acc_ref[...] += _dot(vtl_ref[chunk], p_ref[slot])
        return l + p_sum

    p = jnp.exp2(_dot(kc_ref[...], q2_ref[...]))
    acc_ref[...] = _dot(vtc_ref[0], p.astype(BF16))
    l = _chunk_pipeline(n_chunks, produce_unshifted, consume_unshifted,
                        jnp.sum(p, axis=0, keepdims=True))
    acc = acc_ref[...]
    _finish_heads(acc, l, lam_ref, g_ref, o_ref, lam_init)
    in_range = jnp.minimum(
        jnp.min(jnp.where((l > SUM_MIN) & (l < SUM_MAX), 1.0, 0.0)),
        jnp.min(jnp.where(jnp.abs(acc) < ACC_MAX, 1.0, 0.0)))

    @pl.when(in_range < 0.5)
    def _():
        def shifted_chunk(chunk, carry):
            m, l = carry
            s = _dot(key_rows(chunk), q2_ref[...])
            m_new = jnp.maximum(m, jnp.max(s, axis=0, keepdims=True))
            alpha = jnp.exp2(m - m_new)
            p = jnp.exp2(s - m_new)
            acc_ref[...] = alpha * acc_ref[...] + _dot(vtl_ref[chunk], p.astype(BF16))
            return m_new, alpha * l + jnp.sum(p, axis=0, keepdims=True)

        m, l, acc = _first_chunk(kc_ref[...], vtc_ref[0], q2_ref[...])
        acc_ref[...] = acc
        _, l = lax.fori_loop(0, n_chunks, shifted_chunk, (m, l))
        _finish_heads(acc_ref[...], l, lam_ref, g_ref, o_ref, lam_init)


def _attention(q, k, vt, lam_params, subln_g, lam_init, n_batch, seq, ctx_len, o_prev=None):
    t_lat = n_batch * seq
    ctx_q = o_prev is not None
    assert TKV % ctx_len == 0 and t_lat % TKV == 0 and seq % (2 * TKV) == 0 and seq % TQ == 0
    per_chunk = TKV // ctx_len
    kc_spec = pl.BlockSpec((ctx_len, HEAD_W), lambda b, h, i: (t_lat // ctx_len + b, h))
    vtc_spec = pl.BlockSpec(
        (1, HEAD_W, ctx_len), lambda b, h, i: (t_lat // TKV + b // per_chunk, h, b % per_chunk))
    small = [pl.BlockSpec((4, DA_HEAD_DIM), lambda b, h, i: (0, 0)),
             pl.BlockSpec((1, HEAD_W), lambda b, h, i: (0, 0))]
    if ctx_q:
        tq, n_q = ctx_len, 1
        q_map = lambda b, h, i: (t_lat // ctx_len + b, h)
        in_specs = small + [pl.BlockSpec((tq, HEAD_W), q_map), kc_spec, vtc_spec,
                            pl.BlockSpec(memory_space=pl.ANY)]
        args = (lam_params, subln_g, q, k, vt, o_prev)
        aliases = {5: 0}
        kernel = functools.partial(_attn_ctx_kernel, lam_init=lam_init)
        scratch = []
    else:
        tq, n_q, n_chunks = TQ, seq // TQ, seq // TKV
        q_map = lambda b, h, i: (b * n_q + i, h)
        in_specs = small + [
            pl.BlockSpec((tq, HEAD_W), q_map), kc_spec, vtc_spec,
            pl.BlockSpec((seq, HEAD_W), lambda b, h, i: (b, h)),
            pl.BlockSpec((n_chunks, HEAD_W, TKV), lambda b, h, i: (b, h, 0))]
        args = (lam_params, subln_g, q, k, vt, k, vt)
        aliases = {}
        kernel = functools.partial(_attn_kernel, n_chunks=n_chunks, lam_init=lam_init)
        scratch = [pltpu.VMEM((HEAD_W, 2 * tq), BF16), pltpu.VMEM((2, TKV, 2 * tq), BF16),
                   pltpu.VMEM((HEAD_W, 2 * tq), F32)]
    return pl.pallas_call(
        kernel,
        out_shape=jax.ShapeDtypeStruct(q.shape, BF16),
        grid=(n_batch, DA_HEADS, n_q),
        in_specs=in_specs,
        out_specs=pl.BlockSpec((tq, HEAD_W), q_map),
        scratch_shapes=scratch,
        input_output_aliases=aliases,
        compiler_params=pltpu.CompilerParams(
            dimension_semantics=("parallel", "parallel", "arbitrary"),
            vmem_limit_bytes=V7X_VMEM_LIMIT),
        name="diff_attention_ctx" if ctx_q else "diff_attention",
    )(*args)


def _sgu_kernel(x_ref, mod_ref, g_ref, w_ref, b_ref, lng_ref, lnb_ref, ws_ref, bst_ref, o_ref):
    sh, sc, _ = _mod_slices(mod_ref, 0)
    h = (_rms(x_ref[...], g_ref[0:1, :]) * (1.0 + sc) + sh).astype(BF16)

    def gelu(z):
        return 0.5 * z * (1.0 + lax.erf(z * (2.0 ** -0.5)))

    zv = gelu(_dot(h, w_ref[:, SGU_HALF:]) + b_ref[:, SGU_HALF:])
    mu = jnp.mean(zv, axis=-1, keepdims=True)
    zc = zv - mu
    var = jnp.mean(zc * zc, axis=-1, keepdims=True)
    vn = (zc * lax.rsqrt(var + EPS) * lng_ref[...] + lnb_ref[...]).astype(BF16)
    zu = gelu(_dot(h, w_ref[:, :SGU_HALF]) + b_ref[:, :SGU_HALF])
    bst = bst_ref[...]
    for n in range(TM // CHUNK):
        rows = slice(n * CHUNK, (n + 1) * CHUNK)
        for g in range(SGU_GROUPS):
            cols = slice(g * SGU_GROUP_W, (g + 1) * SGU_GROUP_W)
            sv = _dot(ws_ref[g], vn[rows, cols]) + bst[:, g:g + 1]
            o_ref[rows, cols] = (zu[rows, cols] * sv).astype(BF16)


def _sgu_gate(x_all, mod, norm_g, w_in, b_in, ln_g, ln_b, w_s, b_s_t, layer, n_tiles,
              tiles_per_batch, n_batch):
    x_spec, mod_spec, g_spec = _tile_specs(layer, tiles_per_batch, n_batch)
    return pl.pallas_call(
        _sgu_kernel,
        out_shape=jax.ShapeDtypeStruct((n_tiles * TM, SGU_HALF), BF16),
        grid=(n_tiles,),
        in_specs=[x_spec, mod_spec, g_spec, _resident((D_MODEL, 2 * SGU_HALF)),
                  _resident((1, 2 * SGU_HALF)), _resident((1, SGU_HALF)), _resident((1, SGU_HALF)),
                  _resident((SGU_GROUPS, CHUNK, CHUNK)), _resident((CHUNK, SGU_GROUPS))],
        out_specs=pl.BlockSpec((TM, SGU_HALF), lambda t: (t, 0)),
        compiler_params=pltpu.CompilerParams(
            dimension_semantics=("parallel",), vmem_limit_bytes=V7X_VMEM_LIMIT),
        name="sgu_gate",
    )(x_all, mod, norm_g, w_in, b_in, ln_g, ln_b, w_s, b_s_t)


def _post_kernel(x_ref, u_ref, mod_ref, g_ref, wu_ref, w1_ref, w2_ref, o_ref):
    _, _, gt_a = _mod_slices(mod_ref, 0)
    sh_f, sc_f, gt_f = _mod_slices(mod_ref, 3)
    groups = [slice(r * POST_ROWS, (r + 1) * POST_ROWS) for r in range(TM // POST_ROWS)]
    ys = [_dot(u_ref[rows, :], wu_ref[...]) for rows in groups]
    x1s = [x_ref[rows, :] + gt_a * _rms(y, g_ref[1:2, :]) for rows, y in zip(groups, ys)]
    hs = [(_rms(x1, g_ref[2:3, :]) * (1.0 + sc_f) + sh_f).astype(BF16) for x1 in x1s]
    for rows, x1, h in zip(groups, x1s, hs):
        f = jnp.zeros(x1.shape, F32)
        for c in range(MLP_HIDDEN // HID_TN):
            a = jnp.maximum(_dot(h, w1_ref[:, c * HID_TN:(c + 1) * HID_TN]), 0.0)
            f = f + _dot((a * a).astype(BF16), w2_ref[c * HID_TN:(c + 1) * HID_TN, :])
        o_ref[rows, :] = x1 + gt_f * _rms(f, g_ref[3:4, :])


def _post_mixer(x_all, u, mod, norm_g, w_u, w1, w2, layer, n_tiles, tiles_per_batch, n_batch):
    x_spec, mod_spec, g_spec = _tile_specs(layer, tiles_per_batch, n_batch)
    ku = u.shape[1]
    return pl.pallas_call(
        _post_kernel,
        out_shape=jax.ShapeDtypeStruct((n_tiles * TM, D_MODEL), F32),
        grid=(n_tiles,),
        in_specs=[x_spec, pl.BlockSpec((TM, ku), lambda t: (t, 0)), mod_spec, g_spec,
                  _resident((ku, D_MODEL)), _resident((D_MODEL, MLP_HIDDEN)),
                  _resident((MLP_HIDDEN, D_MODEL))],
        out_specs=pl.BlockSpec((TM, D_MODEL), lambda t: (t, 0)),
        compiler_params=pltpu.CompilerParams(
            dimension_semantics=("parallel",), vmem_limit_bytes=V7X_VMEM_LIMIT),
        name="post_mixer_mlp",
    )(x_all, u, mod, norm_g, w_u, w1, w2)


def _rope_table(seq):
    n_rows = seq // GRID_W
    rows = jnp.broadcast_to(jnp.arange(n_rows, dtype=F32)[:, None], (n_rows, GRID_W)).reshape(-1)
    cols = jnp.broadcast_to(jnp.arange(GRID_W, dtype=F32)[None, :], (n_rows, GRID_W)).reshape(-1)
    inv_freq = ROPE_THETA ** (-jnp.arange(ROPE_PAIRS, dtype=F32) / ROPE_PAIRS)
    ang_r = rows[:, None] * inv_freq
    ang_c = cols[:, None] * inv_freq
    ang = jnp.concatenate([ang_r, ang_r, ang_c, ang_c] * 2, axis=-1)
    cos, sin = jnp.cos(ang), jnp.sin(ang)
    low_half = (jnp.arange(HEAD_W) % (2 * ROPE_PAIRS)) < ROPE_PAIRS
    tab = jnp.concatenate(
        [cos, jnp.where(low_half, -sin, 0.0), jnp.where(low_half, 0.0, sin)], axis=-1)
    ident = jnp.concatenate(
        [jnp.ones((TM, HEAD_W), F32), jnp.zeros((TM, 2 * HEAD_W), F32)], axis=-1)
    return jnp.concatenate([tab, ident], axis=0)


def kernel(x, c, ctx, c_ctx, w_mod, b_mod, norm_g, da_w_qkv, da_w_o, da_lambda, da_subln_g,
           sgu_w_in, sgu_b_in, sgu_ln_g, sgu_ln_b, sgu_w_s, sgu_b_s, sgu_w_out, mlp_w1, mlp_w2):
    n_batch, seq, d = x.shape
    ctx_len = ctx.shape[1]
    assert d == D_MODEL and seq % TM == 0 and (n_batch * ctx_len) % TM == 0
    assert n_batch < MOD_ROWS and TM % TKV == 0 and TM % CHUNK == 0 and ctx_len % CHUNK == 0
    tiles_per_batch = seq // TM
    lat_tiles = n_batch * tiles_per_batch
    all_tiles = lat_tiles + n_batch * ctx_len // TM

    cvec = jnp.concatenate(
        [c, c_ctx[None, :], jnp.zeros((MOD_ROWS - n_batch - 1, d), F32)], axis=0)
    mod = _modulation(cvec, w_mod, b_mod)
    rope_tab = _rope_table(seq)
    x_all = jnp.concatenate([x.reshape(n_batch * seq, d), ctx.reshape(n_batch * ctx_len, d)], axis=0)

    for i in range(DEPTH):
        j = i // 2
        later_attn = any(l % 2 == 0 for l in range(i + 1, DEPTH))
        w1 = mlp_w1[i].astype(BF16)
        w2 = mlp_w2[i].astype(BF16)
        if i % 2 == 0:
            lam_init = 0.8 - 0.6 * math.exp(-0.3 * i)
            g_head = da_subln_g[j].reshape(1, HEAD_W)
            q, k, vt = _qkv_project(x_all, mod, norm_g, da_w_qkv[j].astype(BF16), rope_tab, i,
                                    all_tiles, tiles_per_batch, n_batch)
            u = _attention(q, k, vt, da_lambda[j], g_head, lam_init, n_batch, seq, ctx_len)
            if later_attn:
                u = _attention(q, k, vt, da_lambda[j], g_head, lam_init, n_batch, seq, ctx_len,
                               o_prev=u)
            w_u = da_w_o[j].astype(BF16)
            n_tiles = all_tiles if later_attn else lat_tiles
        else:
            n_tiles = all_tiles if later_attn else lat_tiles
            u = _sgu_gate(x_all, mod, norm_g, sgu_w_in[j].astype(BF16),
                          sgu_b_in[j].reshape(1, -1), sgu_ln_g[j].reshape(1, -1),
                          sgu_ln_b[j].reshape(1, -1), sgu_w_s[j].astype(BF16), sgu_b_s[j].T,
                          i, n_tiles, tiles_per_batch, n_batch)
            w_u = sgu_w_out[j].astype(BF16)
        x_all = _post_mixer(x_all, u, mod, norm_g, w_u, w1, w2, i, n_tiles, tiles_per_batch,
                            n_batch)
    return x_all[:n_batch * seq].reshape(n_batch, seq, d)
```

```python
import functools
import math

import jax
import jax.numpy as jnp
from jax import lax
from jax.experimental import pallas as pl
from jax.experimental.pallas import tpu as pltpu

D_MODEL = 1024
DEPTH = 4
GRID_W = 64
N_MOD = 6
EPS = 1e-6
DA_HEADS = 8
DA_HEAD_DIM = 64
HEAD_W = 2 * DA_HEAD_DIM
ROPE_THETA = 10000.0
ROPE_PAIRS = 16
SGU_HALF = 2048
SGU_GROUPS = 8
SGU_GROUP_W = SGU_HALF // SGU_GROUPS
CHUNK = 128
MLP_HIDDEN = 4 * D_MODEL

TM = 512
TQ = 1024
ATTN_UNROLL = 4
TKV = 512
MOD_ROWS = 8
MOD_TN = 1536
HID_TN = 1024
POST_ROWS = 256
V7X_VMEM_LIMIT = 56 * 1024 * 1024
Q_SCALE = DA_HEAD_DIM ** -0.5 * math.log2(math.e)
SUM_MIN, SUM_MAX, SQUARES_MAX = 2.0 ** -60, 2.0 ** 60, 2.0 ** 120

BF16 = jnp.bfloat16
F32 = jnp.float32


def _dot(a, b):
    return jnp.dot(a, b, preferred_element_type=F32)


def _rms(x, g):
    return x * lax.rsqrt(jnp.mean(x * x, axis=-1, keepdims=True) + EPS) * g


def _resident(shape):
    return pl.BlockSpec(shape, lambda *_: (0,) * len(shape), pipeline_mode=pl.Buffered(1))


def _token_params(semantics=("parallel",)):
    return pltpu.CompilerParams(dimension_semantics=semantics, vmem_limit_bytes=V7X_VMEM_LIMIT)


def _mod_kernel(cv_ref, w_ref, b_ref, o_ref):
    cv = cv_ref[...]
    s = cv / (1.0 + jnp.exp(-cv))
    o_ref[...] = _dot(s.astype(BF16), w_ref[...].astype(BF16)) + b_ref[...]


def _modulation(cvec, w_mod, b_mod):
    n = N_MOD * D_MODEL
    out = pl.pallas_call(
        _mod_kernel,
        out_shape=jax.ShapeDtypeStruct((DEPTH, MOD_ROWS, n), F32),
        grid=(DEPTH, n // MOD_TN),
        in_specs=[
            pl.BlockSpec((MOD_ROWS, D_MODEL), lambda i, j: (0, 0)),
            pl.BlockSpec((None, D_MODEL, MOD_TN), lambda i, j: (i, 0, j)),
            pl.BlockSpec((None, 1, MOD_TN), lambda i, j: (i, 0, j)),
        ],
        out_specs=pl.BlockSpec((None, MOD_ROWS, MOD_TN), lambda i, j: (i, 0, j)),
        compiler_params=_token_params(("parallel", "parallel")),
        name="modulation",
    )(cvec, w_mod, b_mod.reshape(DEPTH, 1, n))
    return out.reshape(DEPTH, MOD_ROWS, 1, n)


def _mod_slices(mod_ref, first):
    m = mod_ref[...]
    return [m[:, (first + k) * D_MODEL:(first + k + 1) * D_MODEL] for k in range(3)]


def _tile_specs(layer, mod_row):
    x_spec = pl.BlockSpec((TM, D_MODEL), lambda t: (t, 0))
    mod_spec = pl.BlockSpec((None, None, 1, N_MOD * D_MODEL), lambda t: (layer, mod_row(t), 0, 0))
    g_spec = pl.BlockSpec((None, 4, D_MODEL), lambda t: (layer, 0, 0))
    return x_spec, mod_spec, g_spec


def _qkv_kernel(x_ref, mod_ref, g_ref, w_ref, tab_ref, q_ref, k_ref, vt_ref):
    sh, sc, _ = _mod_slices(mod_ref, 0)
    h = (_rms(x_ref[...], g_ref[0:1, :]) * (1.0 + sc) + sh).astype(BF16)
    cos = tab_ref[:, 0:HEAD_W]
    sin_up = tab_ref[:, HEAD_W:2 * HEAD_W]
    sin_dn = tab_ref[:, 2 * HEAD_W:3 * HEAD_W]
    for part, out_ref, scale in ((0, q_ref, Q_SCALE), (1, k_ref, None)):
        y = _dot(h, w_ref[:, part * D_MODEL:(part + 1) * D_MODEL])
        for hd in range(DA_HEADS):
            ys = y[:, hd * HEAD_W:(hd + 1) * HEAD_W]
            r = (ys * cos + pltpu.roll(ys, HEAD_W - ROPE_PAIRS, 1) * sin_up
                 + pltpu.roll(ys, ROPE_PAIRS, 1) * sin_dn)
            if scale is not None:
                r = r * scale
            out_ref[:, hd * HEAD_W:(hd + 1) * HEAD_W] = r.astype(BF16)
    v = _dot(h, w_ref[:, 2 * D_MODEL:3 * D_MODEL])
    for c in range(TM // TKV):
        vt_ref[c] = v[c * TKV:(c + 1) * TKV, :].T.astype(BF16)


def _qkv_project(x2d, mod, norm_g, w_qkv, rope_tab, layer, mod_row, rope_block):
    t = x2d.shape[0]
    x_spec, mod_spec, g_spec = _tile_specs(layer, mod_row)
    tab_spec = pl.BlockSpec((TM, 3 * HEAD_W), lambda i: (rope_block(i), 0))
    return pl.pallas_call(
        _qkv_kernel,
        out_shape=(
            jax.ShapeDtypeStruct((t, D_MODEL), BF16),
            jax.ShapeDtypeStruct((t, D_MODEL), BF16),
            jax.ShapeDtypeStruct((t // TKV, D_MODEL, TKV), BF16),
        ),
        grid=(t // TM,),
        in_specs=[x_spec, mod_spec, g_spec, _resident((D_MODEL, 3 * D_MODEL)), tab_spec],
        out_specs=(
            pl.BlockSpec((TM, D_MODEL), lambda i: (i, 0)),
            pl.BlockSpec((TM, D_MODEL), lambda i: (i, 0)),
            pl.BlockSpec((TM // TKV, D_MODEL, TKV), lambda i: (i, 0, 0)),
        ),
        compiler_params=_token_params(),
        name="qkv_project",
    )(x2d, mod, norm_g, w_qkv, rope_tab)


def _stack_query_maps(q_ref):
    qt = q_ref[...].astype(F32).T
    row = lax.broadcasted_iota(jnp.int32, qt.shape, 0)
    return jnp.concatenate(
        [jnp.where(row < DA_HEAD_DIM, qt, 0.0), jnp.where(row >= DA_HEAD_DIM, qt, 0.0)],
        axis=1).astype(BF16)


def _first_chunk(k, vt, q2):
    s = _dot(k, q2)
    m = jnp.max(s, axis=0, keepdims=True)
    p = jnp.exp2(s - m)
    return m, jnp.sum(p, axis=0, keepdims=True), _dot(vt, p.astype(BF16))


def _finish_heads(acc, l, lam_ref, g_ref, o_ref, lam_init):
    tq = o_ref.shape[0]
    lp = lam_ref[...]
    lam = (jnp.exp(jnp.sum(lp[0:1, :] * lp[1:2, :], axis=-1, keepdims=True))
           - jnp.exp(jnp.sum(lp[2:3, :] * lp[3:4, :], axis=-1, keepdims=True)) + lam_init)
    inv = 1.0 / l
    ot = acc[:, :tq] * inv[:, :tq] - acc[:, tq:] * (lam * inv[:, tq:])
    squares = jnp.sum(ot * ot, axis=0, keepdims=True)
    otn = ot * lax.rsqrt(squares * (1.0 / HEAD_W) + EPS)
    o_ref[...] = (otn.T * g_ref[...] * (1.0 - lam_init)).astype(BF16)
    return squares


def _attn_ctx_kernel(lam_ref, g_ref, q_ref, kc_ref, vtc_ref, o_ref, *, lam_init):
    _, l, acc = _first_chunk(kc_ref[...], vtc_ref[0], _stack_query_maps(q_ref))
    _finish_heads(acc, l, lam_ref, g_ref, o_ref, lam_init)


def _chunk_pipeline(n_chunks, produce, consume, carry):
    def run(first, count, last, carry, out):
        for u in range(count):
            nxt = None if last and u == count - 1 else produce((u + 1) % 2, first + u + 1)
            carry = consume(u % 2, first + u, out, carry)
            out = nxt
        return carry, out

    trips = (n_chunks - 1) // ATTN_UNROLL
    carry, out = lax.fori_loop(
        0, trips, lambda j, c: run(j * ATTN_UNROLL, ATTN_UNROLL, False, *c),
        (carry, produce(0, 0)))
    done = trips * ATTN_UNROLL
    return run(done, n_chunks - done, True, carry, out)[0]


def _attn_kernel(lam_ref, g_ref, q_ref, kc_ref, vtc_ref, kl_ref, vtl_ref, o_ref,
                 q2_ref, p_ref, acc_ref, *, n_chunks, lam_init):
    q2_ref[...] = _stack_query_maps(q_ref)

    def key_rows(chunk):
        off = chunk * TKV if isinstance(chunk, int) else pl.multiple_of(chunk * TKV, TKV)
        return kl_ref[pl.ds(off, TKV), :]

    def produce_unshifted(slot, chunk):
        p = jnp.exp2(_dot(key_rows(chunk), q2_ref[...]))
        p_ref[slot] = p.astype(BF16)
        return jnp.sum(p, axis=0, keepdims=True)

    def consume_unshifted(slot, chunk, p_sum, l):
        acc_ref[...] += _dot(vtl_ref[chunk], p_ref[slot])
        return l + p_sum

    p = jnp.exp2(_dot(kc_ref[...], q2_ref[...]))
    acc_ref[...] = _dot(vtc_ref[0], p.astype(BF16))
    l = _chunk_pipeline(n_chunks, produce_unshifted, consume_unshifted,
                        jnp.sum(p, axis=0, keepdims=True))
    squares = _finish_heads(acc_ref[...], l, lam_ref, g_ref, o_ref, lam_init)
    in_range = jnp.minimum(
        jnp.min(jnp.where((l > SUM_MIN) & (l < SUM_MAX), 1.0, 0.0)),
        jnp.min(jnp.where(squares < SQUARES_MAX, 1.0, 0.0)))

    @pl.when(in_range < 0.5)
    def _():
        def shifted_chunk(chunk, carry):
            m, l = carry
            s = _dot(key_rows(chunk), q2_ref[...])
            m_new = jnp.maximum(m, jnp.max(s, axis=0, keepdims=True))
            alpha = jnp.exp2(m - m_new)
            p = jnp.exp2(s - m_new)
            acc_ref[...] = alpha * acc_ref[...] + _dot(vtl_ref[chunk], p.astype(BF16))
            return m_new, alpha * l + jnp.sum(p, axis=0, keepdims=True)

        m, l, acc = _first_chunk(kc_ref[...], vtc_ref[0], q2_ref[...])
        acc_ref[...] = acc
        _, l = lax.fori_loop(0, n_chunks, shifted_chunk, (m, l))
        _finish_heads(acc_ref[...], l, lam_ref, g_ref, o_ref, lam_init)


def _attention(q, kc, vtc, lam_params, subln_g, lam_init, n_batch, ctx_len, k=None, vt=None):
    latent = k is not None
    seq = q.shape[0] // n_batch
    assert TKV % ctx_len == 0 and kc.shape[0] % TKV == 0
    per_chunk = TKV // ctx_len
    kc_spec = pl.BlockSpec((ctx_len, HEAD_W), lambda b, h, i: (b, h))
    vtc_spec = pl.BlockSpec((1, HEAD_W, ctx_len), lambda b, h, i: (b // per_chunk, h, b % per_chunk))
    small = [pl.BlockSpec((4, DA_HEAD_DIM), lambda b, h, i: (0, 0)),
             pl.BlockSpec((1, HEAD_W), lambda b, h, i: (0, 0))]
    if latent:
        assert seq % (2 * TKV) == 0 and seq % TQ == 0
        tq, n_q, n_chunks = TQ, seq // TQ, seq // TKV
        in_specs = [pl.BlockSpec((seq, HEAD_W), lambda b, h, i: (b, h)),
                    pl.BlockSpec((n_chunks, HEAD_W, TKV), lambda b, h, i: (b, h, 0))]
        args = (k, vt)
        body = functools.partial(_attn_kernel, n_chunks=n_chunks, lam_init=lam_init)
        scratch = [pltpu.VMEM((HEAD_W, 2 * tq), BF16), pltpu.VMEM((2, TKV, 2 * tq), BF16),
                   pltpu.VMEM((HEAD_W, 2 * tq), F32)]
    else:
        assert seq == ctx_len
        tq, n_q = ctx_len, 1
        in_specs, args, scratch = [], (), []
        body = functools.partial(_attn_ctx_kernel, lam_init=lam_init)
    q_spec = pl.BlockSpec((tq, HEAD_W), lambda b, h, i: (b * n_q + i, h))
    return pl.pallas_call(
        body,
        out_shape=jax.ShapeDtypeStruct(q.shape, BF16),
        grid=(n_batch, DA_HEADS, n_q),
        in_specs=small + [q_spec, kc_spec, vtc_spec] + in_specs,
        out_specs=q_spec,
        scratch_shapes=scratch,
        compiler_params=_token_params(("parallel", "parallel", "arbitrary")),
        name="diff_attention" if latent else "diff_attention_ctx",
    )(lam_params, subln_g, q, kc, vtc, *args)


def _sgu_kernel(x_ref, mod_ref, g_ref, w_ref, b_ref, lng_ref, lnb_ref, ws_ref, bst_ref, o_ref):
    sh, sc, _ = _mod_slices(mod_ref, 0)
    h = (_rms(x_ref[...], g_ref[0:1, :]) * (1.0 + sc) + sh).astype(BF16)

    def gelu(z):
        return 0.5 * z * (1.0 + lax.erf(z * (2.0 ** -0.5)))

    zv = gelu(_dot(h, w_ref[:, SGU_HALF:]) + b_ref[:, SGU_HALF:])
    mu = jnp.mean(zv, axis=-1, keepdims=True)
    zc = zv - mu
    var = jnp.mean(zc * zc, axis=-1, keepdims=True)
    vn = (zc * lax.rsqrt(var + EPS) * lng_ref[...] + lnb_ref[...]).astype(BF16)
    zu = gelu(_dot(h, w_ref[:, :SGU_HALF]) + b_ref[:, :SGU_HALF])
    bst = bst_ref[...]
    for n in range(TM // CHUNK):
        rows = slice(n * CHUNK, (n + 1) * CHUNK)
        for g in range(SGU_GROUPS):
            cols = slice(g * SGU_GROUP_W, (g + 1) * SGU_GROUP_W)
            sv = _dot(ws_ref[g], vn[rows, cols]) + bst[:, g:g + 1]
            o_ref[rows, cols] = (zu[rows, cols] * sv).astype(BF16)


def _sgu_gate(x2d, mod, norm_g, w_in, b_in, ln_g, ln_b, w_s, b_s_t, layer, mod_row):
    t = x2d.shape[0]
    x_spec, mod_spec, g_spec = _tile_specs(layer, mod_row)
    return pl.pallas_call(
        _sgu_kernel,
        out_shape=jax.ShapeDtypeStruct((t, SGU_HALF), BF16),
        grid=(t // TM,),
        in_specs=[x_spec, mod_spec, g_spec, _resident((D_MODEL, 2 * SGU_HALF)),
                  _resident((1, 2 * SGU_HALF)), _resident((1, SGU_HALF)), _resident((1, SGU_HALF)),
                  _resident((SGU_GROUPS, CHUNK, CHUNK)), _resident((CHUNK, SGU_GROUPS))],
        out_specs=pl.BlockSpec((TM, SGU_HALF), lambda i: (i, 0)),
        compiler_params=_token_params(),
        name="sgu_gate",
    )(x2d, mod, norm_g, w_in, b_in, ln_g, ln_b, w_s, b_s_t)


def _post_kernel(x_ref, u_ref, mod_ref, g_ref, wu_ref, w1_ref, w2_ref, o_ref):
    _, _, gt_a = _mod_slices(mod_ref, 0)
    sh_f, sc_f, gt_f = _mod_slices(mod_ref, 3)
    groups = [slice(r * POST_ROWS, (r + 1) * POST_ROWS) for r in range(TM // POST_ROWS)]
    ys = [_dot(u_ref[rows, :], wu_ref[...]) for rows in groups]
    x1s = [x_ref[rows, :] + gt_a * _rms(y, g_ref[1:2, :]) for rows, y in zip(groups, ys)]
    hs = [(_rms(x1, g_ref[2:3, :]) * (1.0 + sc_f) + sh_f).astype(BF16) for x1 in x1s]
    for rows, x1, h in zip(groups, x1s, hs):
        f = jnp.zeros(x1.shape, F32)
        for c in range(MLP_HIDDEN // HID_TN):
            a = jnp.maximum(_dot(h, w1_ref[:, c * HID_TN:(c + 1) * HID_TN]), 0.0)
            f = f + _dot((a * a).astype(BF16), w2_ref[c * HID_TN:(c + 1) * HID_TN, :])
        o_ref[rows, :] = x1 + gt_f * _rms(f, g_ref[3:4, :])


def _post_mixer(x2d, u, mod, norm_g, w_u, w1, w2, layer, mod_row):
    t = x2d.shape[0]
    x_spec, mod_spec, g_spec = _tile_specs(layer, mod_row)
    ku = u.shape[1]
    return pl.pallas_call(
        _post_kernel,
        out_shape=jax.ShapeDtypeStruct((t, D_MODEL), F32),
        grid=(t // TM,),
        in_specs=[x_spec, pl.BlockSpec((TM, ku), lambda i: (i, 0)), mod_spec, g_spec,
                  _resident((ku, D_MODEL)), _resident((D_MODEL, MLP_HIDDEN)),
                  _resident((MLP_HIDDEN, D_MODEL))],
        out_specs=pl.BlockSpec((TM, D_MODEL), lambda i: (i, 0)),
        compiler_params=_token_params(),
        name="post_mixer_mlp",
    )(x2d, u, mod, norm_g, w_u, w1, w2)


def _rope_table(seq):
    n_rows = seq // GRID_W
    rows = jnp.broadcast_to(jnp.arange(n_rows, dtype=F32)[:, None], (n_rows, GRID_W)).reshape(-1)
    cols = jnp.broadcast_to(jnp.arange(GRID_W, dtype=F32)[None, :], (n_rows, GRID_W)).reshape(-1)
    inv_freq = ROPE_THETA ** (-jnp.arange(ROPE_PAIRS, dtype=F32) / ROPE_PAIRS)
    ang_r = rows[:, None] * inv_freq
    ang_c = cols[:, None] * inv_freq
    ang = jnp.concatenate([ang_r, ang_r, ang_c, ang_c] * 2, axis=-1)
    cos, sin = jnp.cos(ang), jnp.sin(ang)
    low_half = (jnp.arange(HEAD_W) % (2 * ROPE_PAIRS)) < ROPE_PAIRS
    tab = jnp.concatenate(
        [cos, jnp.where(low_half, -sin, 0.0), jnp.where(low_half, 0.0, sin)], axis=-1)
    ident = jnp.concatenate(
        [jnp.ones((TM, HEAD_W), F32), jnp.zeros((TM, 2 * HEAD_W), F32)], axis=-1)
    return jnp.concatenate([tab, ident], axis=0)


def kernel(x, c, ctx, c_ctx, w_mod, b_mod, norm_g, da_w_qkv, da_w_o, da_lambda, da_subln_g,
           sgu_w_in, sgu_b_in, sgu_ln_g, sgu_ln_b, sgu_w_s, sgu_b_s, sgu_w_out, mlp_w1, mlp_w2):
    n_batch, seq, d = x.shape
    ctx_len = ctx.shape[1]
    assert d == D_MODEL and seq % TM == 0 and (n_batch * ctx_len) % TM == 0
    assert n_batch < MOD_ROWS and TM % TKV == 0 and TM % CHUNK == 0 and ctx_len % CHUNK == 0
    tiles_per_batch = seq // TM

    cvec = jnp.concatenate(
        [c, c_ctx[None, :], jnp.zeros((MOD_ROWS - n_batch - 1, d), F32)], axis=0)
    mod = _modulation(cvec, w_mod, b_mod)
    rope_tab = _rope_table(seq)

    lat = (x.reshape(n_batch * seq, d), lambda t: t // tiles_per_batch, lambda t: t % tiles_per_batch)
    con = (ctx.reshape(n_batch * ctx_len, d), lambda t: n_batch, lambda t: tiles_per_batch)

    for i in range(DEPTH):
        j = i // 2
        attn_layer = i % 2 == 0
        later_attn = any(l % 2 == 0 for l in range(i + 1, DEPTH))
        w1 = mlp_w1[i].astype(BF16)
        w2 = mlp_w2[i].astype(BF16)
        if attn_layer:
            lam_init = 0.8 - 0.6 * math.exp(-0.3 * i)
            g_head = da_subln_g[j].reshape(1, HEAD_W)
            w_qkv = da_w_qkv[j].astype(BF16)
            w_u = da_w_o[j].astype(BF16)
            q, k, vt = _qkv_project(lat[0], mod, norm_g, w_qkv, rope_tab, i, lat[1], lat[2])
            qc, kc, vtc = _qkv_project(con[0], mod, norm_g, w_qkv, rope_tab, i, con[1], con[2])
            attend = functools.partial(_attention, lam_params=da_lambda[j], subln_g=g_head,
                                       lam_init=lam_init, n_batch=n_batch, ctx_len=ctx_len)
            u_lat = attend(q, kc, vtc, k=k, vt=vt)
            u_con = attend(qc, kc, vtc) if later_attn else None
        else:
            w_u = sgu_w_out[j].astype(BF16)
            gate = functools.partial(
                _sgu_gate, mod=mod, norm_g=norm_g, w_in=sgu_w_in[j].astype(BF16),
                b_in=sgu_b_in[j].reshape(1, -1), ln_g=sgu_ln_g[j].reshape(1, -1),
                ln_b=sgu_ln_b[j].reshape(1, -1), w_s=sgu_w_s[j].astype(BF16),
                b_s_t=sgu_b_s[j].T, layer=i)
            u_lat = gate(lat[0], mod_row=lat[1])
            u_con = gate(con[0], mod_row=con[1]) if later_attn else None
        lat = (_post_mixer(lat[0], u_lat, mod, norm_g, w_u, w1, w2, i, lat[1]),) + lat[1:]
        if later_attn:
            con = (_post_mixer(con[0], u_con, mod, norm_g, w_u, w1, w2, i, con[1]),) + con[1:]
    return lat[0].reshape(n_batch, seq, d)
```

```python
import functools
import math

import jax
import jax.numpy as jnp
from jax import lax
from jax.experimental import pallas as pl
from jax.experimental.pallas import tpu as pltpu

D_MODEL = 1024
DEPTH = 4
GRID_W = 64
N_MOD = 6
EPS = 1e-6
DA_HEADS = 8
DA_HEAD_DIM = 64
HEAD_W = 2 * DA_HEAD_DIM
ROPE_THETA = 10000.0
ROPE_PAIRS = 16
SGU_HALF = 2048
SGU_GROUPS = 8
SGU_GROUP_W = SGU_HALF // SGU_GROUPS
CHUNK = 128
MLP_HIDDEN = 4 * D_MODEL

TM = 512
TQ = 2048
ATTN_UNROLL = 4
TKV = 512
MOD_ROWS = 8
MOD_TN = 1536
HID_TN = 1024
POST_ROWS = 256
V7X_VMEM_LIMIT = 56 * 1024 * 1024
Q_SCALE = DA_HEAD_DIM ** -0.5 * math.log2(math.e)
SUM_MIN, SUM_MAX, SQUARES_MAX = 2.0 ** -60, 2.0 ** 60, 2.0 ** 120

BF16 = jnp.bfloat16
F32 = jnp.float32


def _dot(a, b):
    return jnp.dot(a, b, preferred_element_type=F32)


def _rms(x, g):
    return x * lax.rsqrt(jnp.mean(x * x, axis=-1, keepdims=True) + EPS) * g


def _resident(shape):
    return pl.BlockSpec(shape, lambda *_: (0,) * len(shape), pipeline_mode=pl.Buffered(1))


def _token_params(semantics=("parallel",)):
    return pltpu.CompilerParams(dimension_semantics=semantics, vmem_limit_bytes=V7X_VMEM_LIMIT)


def _mod_kernel(cv_ref, w_ref, b_ref, o_ref):
    cv = cv_ref[...]
    s = cv / (1.0 + jnp.exp(-cv))
    o_ref[...] = _dot(s.astype(BF16), w_ref[...].astype(BF16)) + b_ref[...]


def _modulation(cvec, w_mod, b_mod):
    n = N_MOD * D_MODEL
    out = pl.pallas_call(
        _mod_kernel,
        out_shape=jax.ShapeDtypeStruct((DEPTH, MOD_ROWS, n), F32),
        grid=(DEPTH, n // MOD_TN),
        in_specs=[
            pl.BlockSpec((MOD_ROWS, D_MODEL), lambda i, j: (0, 0)),
            pl.BlockSpec((None, D_MODEL, MOD_TN), lambda i, j: (i, 0, j)),
            pl.BlockSpec((None, 1, MOD_TN), lambda i, j: (i, 0, j)),
        ],
        out_specs=pl.BlockSpec((None, MOD_ROWS, MOD_TN), lambda i, j: (i, 0, j)),
        compiler_params=_token_params(("parallel", "parallel")),
        name="modulation",
    )(cvec, w_mod, b_mod.reshape(DEPTH, 1, n))
    return out.reshape(DEPTH, MOD_ROWS, 1, n)


def _mod_slices(mod_ref, first):
    m = mod_ref[...]
    return [m[:, (first + k) * D_MODEL:(first + k + 1) * D_MODEL] for k in range(3)]


def _tile_specs(layer, mod_row):
    x_spec = pl.BlockSpec((TM, D_MODEL), lambda t: (t, 0))
    mod_spec = pl.BlockSpec((None, None, 1, N_MOD * D_MODEL), lambda t: (layer, mod_row(t), 0, 0))
    g_spec = pl.BlockSpec((None, 4, D_MODEL), lambda t: (layer, 0, 0))
    return x_spec, mod_spec, g_spec


def _qkv_kernel(x_ref, mod_ref, g_ref, w_ref, tab_ref, q_ref, k_ref, vt_ref):
    sh, sc, _ = _mod_slices(mod_ref, 0)
    h = (_rms(x_ref[...], g_ref[0:1, :]) * (1.0 + sc) + sh).astype(BF16)
    cos = tab_ref[:, 0:HEAD_W]
    sin_up = tab_ref[:, HEAD_W:2 * HEAD_W]
    sin_dn = tab_ref[:, 2 * HEAD_W:3 * HEAD_W]
    for part, out_ref, scale in ((0, q_ref, Q_SCALE), (1, k_ref, None)):
        y = _dot(h, w_ref[:, part * D_MODEL:(part + 1) * D_MODEL])
        for hd in range(DA_HEADS):
            ys = y[:, hd * HEAD_W:(hd + 1) * HEAD_W]
            r = (ys * cos + pltpu.roll(ys, HEAD_W - ROPE_PAIRS, 1) * sin_up
                 + pltpu.roll(ys, ROPE_PAIRS, 1) * sin_dn)
            if scale is not None:
                r = r * scale
            out_ref[:, hd * HEAD_W:(hd + 1) * HEAD_W] = r.astype(BF16)
    v = _dot(h, w_ref[:, 2 * D_MODEL:3 * D_MODEL])
    for c in range(TM // TKV):
        vt_ref[c] = v[c * TKV:(c + 1) * TKV, :].T.astype(BF16)


def _qkv_project(x2d, mod, norm_g, w_qkv, rope_tab, layer, mod_row, rope_block):
    t = x2d.shape[0]
    x_spec, mod_spec, g_spec = _tile_specs(layer, mod_row)
    tab_spec = pl.BlockSpec((TM, 3 * HEAD_W), lambda i: (rope_block(i), 0))
    return pl.pallas_call(
        _qkv_kernel,
        out_shape=(
            jax.ShapeDtypeStruct((t, D_MODEL), BF16),
            jax.ShapeDtypeStruct((t, D_MODEL), BF16),
            jax.ShapeDtypeStruct((t // TKV, D_MODEL, TKV), BF16),
        ),
        grid=(t // TM,),
        in_specs=[x_spec, mod_spec, g_spec, _resident((D_MODEL, 3 * D_MODEL)), tab_spec],
        out_specs=(
            pl.BlockSpec((TM, D_MODEL), lambda i: (i, 0)),
            pl.BlockSpec((TM, D_MODEL), lambda i: (i, 0)),
            pl.BlockSpec((TM // TKV, D_MODEL, TKV), lambda i: (i, 0, 0)),
        ),
        compiler_params=_token_params(),
        name="qkv_project",
    )(x2d, mod, norm_g, w_qkv, rope_tab)


def _stack_query_maps(q_ref):
    qt = q_ref[...].astype(F32).T
    row = lax.broadcasted_iota(jnp.int32, qt.shape, 0)
    return jnp.concatenate(
        [jnp.where(row < DA_HEAD_DIM, qt, 0.0), jnp.where(row >= DA_HEAD_DIM, qt, 0.0)],
        axis=1).astype(BF16)


def _first_chunk(k, vt, q2):
    s = _dot(k, q2)
    m = jnp.max(s, axis=0, keepdims=True)
    p = jnp.exp2(s - m)
    return m, jnp.sum(p, axis=0, keepdims=True), _dot(vt, p.astype(BF16))


def _finish_heads(acc, l, lam_ref, g_ref, o_ref, lam_init):
    tq = o_ref.shape[0]
    lp = lam_ref[...]
    lam = (jnp.exp(jnp.sum(lp[0:1, :] * lp[1:2, :], axis=-1, keepdims=True))
           - jnp.exp(jnp.sum(lp[2:3, :] * lp[3:4, :], axis=-1, keepdims=True)) + lam_init)
    inv = 1.0 / l
    ot = acc[:, :tq] * inv[:, :tq] - acc[:, tq:] * (lam * inv[:, tq:])
    squares = jnp.sum(ot * ot, axis=0, keepdims=True)
    otn = ot * lax.rsqrt(squares * (1.0 / HEAD_W) + EPS)
    o_ref[...] = (otn.T * g_ref[...] * (1.0 - lam_init)).astype(BF16)
    return squares


def _attn_ctx_kernel(lam_ref, g_ref, q_ref, kc_ref, vtc_ref, o_ref, *, lam_init):
    _, l, acc = _first_chunk(kc_ref[...], vtc_ref[0], _stack_query_maps(q_ref))
    _finish_heads(acc, l, lam_ref, g_ref, o_ref, lam_init)


def _chunk_pipeline(n_chunks, produce, consume, carry):
    def run(first, count, last, carry, out):
        for u in range(count):
            nxt = None if last and u == count - 1 else produce((u + 1) % 2, first + u + 1)
            carry = consume(u % 2, first + u, out, carry)
            out = nxt
        return carry, out

    trips = (n_chunks - 1) // ATTN_UNROLL
    carry, out = lax.fori_loop(
        0, trips, lambda j, c: run(j * ATTN_UNROLL, ATTN_UNROLL, False, *c),
        (carry, produce(0, 0)))
    done = trips * ATTN_UNROLL
    return run(done, n_chunks - done, True, carry, out)[0]


def _attn_kernel(lam_ref, g_ref, q_ref, kc_ref, vtc_ref, kl_ref, vtl_ref, o_ref,
                 q2_ref, p_ref, acc_ref, *, n_chunks, lam_init):
    q2_ref[...] = _stack_query_maps(q_ref)

    def key_rows(chunk):
        off = chunk * TKV if isinstance(chunk, int) else pl.multiple_of(chunk * TKV, TKV)
        return kl_ref[pl.ds(off, TKV), :]

    def produce_unshifted(slot, chunk):
        p = jnp.exp2(_dot(key_rows(chunk), q2_ref[...]))
        p_ref[slot] = p.astype(BF16)
        return jnp.sum(p, axis=0, keepdims=True)

    def consume_unshifted(slot, chunk, p_sum, l):
        acc_ref[...] += _dot(vtl_ref[chunk], p_ref[slot])
        return l + p_sum

    p = jnp.exp2(_dot(kc_ref[...], q2_ref[...]))
    acc_ref[...] = _dot(vtc_ref[0], p.astype(BF16))
    l = _chunk_pipeline(n_chunks, produce_unshifted, consume_unshifted,
                        jnp.sum(p, axis=0, keepdims=True))
    squares = _finish_heads(acc_ref[...], l, lam_ref, g_ref, o_ref, lam_init)
    in_range = jnp.minimum(
        jnp.min(jnp.where((l > SUM_MIN) & (l < SUM_MAX), 1.0, 0.0)),
        jnp.min(jnp.where(squares < SQUARES_MAX, 1.0, 0.0)))

    @pl.when(in_range < 0.5)
    def _():
        def shifted_chunk(chunk, carry):
            m, l = carry
            s = _dot(key_rows(chunk), q2_ref[...])
            m_new = jnp.maximum(m, jnp.max(s, axis=0, keepdims=True))
            alpha = jnp.exp2(m - m_new)
            p = jnp.exp2(s - m_new)
            acc_ref[...] = alpha * acc_ref[...] + _dot(vtl_ref[chunk], p.astype(BF16))
            return m_new, alpha * l + jnp.sum(p, axis=0, keepdims=True)

        m, l, acc = _first_chunk(kc_ref[...], vtc_ref[0], q2_ref[...])
        acc_ref[...] = acc
        _, l = lax.fori_loop(0, n_chunks, shifted_chunk, (m, l))
        _finish_heads(acc_ref[...], l, lam_ref, g_ref, o_ref, lam_init)


def _attention(q, kc, vtc, lam_params, subln_g, lam_init, n_batch, ctx_len, k=None, vt=None):
    latent = k is not None
    seq = q.shape[0] // n_batch
    assert TKV % ctx_len == 0 and kc.shape[0] % TKV == 0
    per_chunk = TKV // ctx_len
    kc_spec = pl.BlockSpec((ctx_len, HEAD_W), lambda b, h, i: (b, h))
    vtc_spec = pl.BlockSpec((1, HEAD_W, ctx_len), lambda b, h, i: (b // per_chunk, h, b % per_chunk))
    small = [pl.BlockSpec((4, DA_HEAD_DIM), lambda b, h, i: (0, 0)),
             pl.BlockSpec((1, HEAD_W), lambda b, h, i: (0, 0))]
    if latent:
        assert seq % (2 * TKV) == 0 and seq % TQ == 0
        tq, n_q, n_chunks = TQ, seq // TQ, seq // TKV
        in_specs = [pl.BlockSpec((seq, HEAD_W), lambda b, h, i: (b, h)),
                    pl.BlockSpec((n_chunks, HEAD_W, TKV), lambda b, h, i: (b, h, 0))]
        args = (k, vt)
        body = functools.partial(_attn_kernel, n_chunks=n_chunks, lam_init=lam_init)
        scratch = [pltpu.VMEM((HEAD_W, 2 * tq), BF16), pltpu.VMEM((2, TKV, 2 * tq), BF16),
                   pltpu.VMEM((HEAD_W, 2 * tq), F32)]
    else:
        assert seq == ctx_len
        tq, n_q = ctx_len, 1
        in_specs, args, scratch = [], (), []
        body = functools.partial(_attn_ctx_kernel, lam_init=lam_init)
    q_spec = pl.BlockSpec((tq, HEAD_W), lambda b, h, i: (b * n_q + i, h))
    return pl.pallas_call(
        body,
        out_shape=jax.ShapeDtypeStruct(q.shape, BF16),
        grid=(n_batch, DA_HEADS, n_q),
        in_specs=small + [q_spec, kc_spec, vtc_spec] + in_specs,
        out_specs=q_spec,
        scratch_shapes=scratch,
        compiler_params=_token_params(("parallel", "parallel", "arbitrary")),
        name="diff_attention" if latent else "diff_attention_ctx",
    )(lam_params, subln_g, q, kc, vtc, *args)


def _sgu_kernel(x_ref, mod_ref, g_ref, w_ref, b_ref, lng_ref, lnb_ref, ws_ref, bst_ref, o_ref):
    sh, sc, _ = _mod_slices(mod_ref, 0)
    h = (_rms(x_ref[...], g_ref[0:1, :]) * (1.0 + sc) + sh).astype(BF16)

    def gelu(z):
        return 0.5 * z * (1.0 + lax.erf(z * (2.0 ** -0.5)))

    zv = gelu(_dot(h, w_ref[:, SGU_HALF:]) + b_ref[:, SGU_HALF:])
    mu = jnp.mean(zv, axis=-1, keepdims=True)
    zc = zv - mu
    var = jnp.mean(zc * zc, axis=-1, keepdims=True)
    vn = (zc * lax.rsqrt(var + EPS) * lng_ref[...] + lnb_ref[...]).astype(BF16)
    zu = gelu(_dot(h, w_ref[:, :SGU_HALF]) + b_ref[:, :SGU_HALF])
    bst = bst_ref[...]
    for n in range(TM // CHUNK):
        rows = slice(n * CHUNK, (n + 1) * CHUNK)
        for g in range(SGU_GROUPS):
            cols = slice(g * SGU_GROUP_W, (g + 1) * SGU_GROUP_W)
            sv = _dot(ws_ref[g], vn[rows, cols]) + bst[:, g:g + 1]
            o_ref[rows, cols] = (zu[rows, cols] * sv).astype(BF16)


def _sgu_gate(x2d, mod, norm_g, w_in, b_in, ln_g, ln_b, w_s, b_s_t, layer, mod_row):
    t = x2d.shape[0]
    x_spec, mod_spec, g_spec = _tile_specs(layer, mod_row)
    return pl.pallas_call(
        _sgu_kernel,
        out_shape=jax.ShapeDtypeStruct((t, SGU_HALF), BF16),
        grid=(t // TM,),
        in_specs=[x_spec, mod_spec, g_spec, _resident((D_MODEL, 2 * SGU_HALF)),
                  _resident((1, 2 * SGU_HALF)), _resident((1, SGU_HALF)), _resident((1, SGU_HALF)),
                  _resident((SGU_GROUPS, CHUNK, CHUNK)), _resident((CHUNK, SGU_GROUPS))],
        out_specs=pl.BlockSpec((TM, SGU_HALF), lambda i: (i, 0)),
        compiler_params=_token_params(),
        name="sgu_gate",
    )(x2d, mod, norm_g, w_in, b_in, ln_g, ln_b, w_s, b_s_t)


def _post_kernel(x_ref, u_ref, mod_ref, g_ref, wu_ref, w1_ref, w2_ref, o_ref):
    _, _, gt_a = _mod_slices(mod_ref, 0)
    sh_f, sc_f, gt_f = _mod_slices(mod_ref, 3)
    groups = [slice(r * POST_ROWS, (r + 1) * POST_ROWS) for r in range(TM // POST_ROWS)]
    ys = [_dot(u_ref[rows, :], wu_ref[...]) for rows in groups]
    x1s = [x_ref[rows, :] + gt_a * _rms(y, g_ref[1:2, :]) for rows, y in zip(groups, ys)]
    hs = [(_rms(x1, g_ref[2:3, :]) * (1.0 + sc_f) + sh_f).astype(BF16) for x1 in x1s]
    for rows, x1, h in zip(groups, x1s, hs):
        f = jnp.zeros(x1.shape, F32)
        for c in range(MLP_HIDDEN // HID_TN):
            a = jnp.maximum(_dot(h, w1_ref[:, c * HID_TN:(c + 1) * HID_TN]), 0.0)
            f = f + _dot((a * a).astype(BF16), w2_ref[c * HID_TN:(c + 1) * HID_TN, :])
        o_ref[rows, :] = x1 + gt_f * _rms(f, g_ref[3:4, :])


def _post_mixer(x2d, u, mod, norm_g, w_u, w1, w2, layer, mod_row):
    t = x2d.shape[0]
    x_spec, mod_spec, g_spec = _tile_specs(layer, mod_row)
    ku = u.shape[1]
    return pl.pallas_call(
        _post_kernel,
        out_shape=jax.ShapeDtypeStruct((t, D_MODEL), F32),
        grid=(t // TM,),
        in_specs=[x_spec, pl.BlockSpec((TM, ku), lambda i: (i, 0)), mod_spec, g_spec,
                  _resident((ku, D_MODEL)), _resident((D_MODEL, MLP_HIDDEN)),
                  _resident((MLP_HIDDEN, D_MODEL))],
        out_specs=pl.BlockSpec((TM, D_MODEL), lambda i: (i, 0)),
        compiler_params=_token_params(),
        name="post_mixer_mlp",
    )(x2d, u, mod, norm_g, w_u, w1, w2)


def _rope_table(seq):
    n_rows = seq // GRID_W
    rows = jnp.broadcast_to(jnp.arange(n_rows, dtype=F32)[:, None], (n_rows, GRID_W)).reshape(-1)
    cols = jnp.broadcast_to(jnp.arange(GRID_W, dtype=F32)[None, :], (n_rows, GRID_W)).reshape(-1)
    inv_freq = ROPE_THETA ** (-jnp.arange(ROPE_PAIRS, dtype=F32) / ROPE_PAIRS)
    ang_r = rows[:, None] * inv_freq
    ang_c = cols[:, None] * inv_freq
    ang = jnp.concatenate([ang_r, ang_r, ang_c, ang_c] * 2, axis=-1)
    cos, sin = jnp.cos(ang), jnp.sin(ang)
    low_half = (jnp.arange(HEAD_W) % (2 * ROPE_PAIRS)) < ROPE_PAIRS
    tab = jnp.concatenate(
        [cos, jnp.where(low_half, -sin, 0.0), jnp.where(low_half, 0.0, sin)], axis=-1)
    ident = jnp.concatenate(
        [jnp.ones((TM, HEAD_W), F32), jnp.zeros((TM, 2 * HEAD_W), F32)], axis=-1)
    return jnp.concatenate([tab, ident], axis=0)


def kernel(x, c, ctx, c_ctx, w_mod, b_mod, norm_g, da_w_qkv, da_w_o, da_lambda, da_subln_g,
           sgu_w_in, sgu_b_in, sgu_ln_g, sgu_ln_b, sgu_w_s, sgu_b_s, sgu_w_out, mlp_w1, mlp_w2):
    n_batch, seq, d = x.shape
    ctx_len = ctx.shape[1]
    assert d == D_MODEL and seq % TM == 0 and (n_batch * ctx_len) % TM == 0
    assert n_batch < MOD_ROWS and TM % TKV == 0 and TM % CHUNK == 0 and ctx_len % CHUNK == 0
    tiles_per_batch = seq // TM

    cvec = jnp.concatenate(
        [c, c_ctx[None, :], jnp.zeros((MOD_ROWS - n_batch - 1, d), F32)], axis=0)
    mod = _modulation(cvec, w_mod, b_mod)
    rope_tab = _rope_table(seq)

    lat = (x.reshape(n_batch * seq, d), lambda t: t // tiles_per_batch, lambda t: t % tiles_per_batch)
    con = (ctx.reshape(n_batch * ctx_len, d), lambda t: n_batch, lambda t: tiles_per_batch)

    for i in range(DEPTH):
        j = i // 2
        attn_layer = i % 2 == 0
        later_attn = any(l % 2 == 0 for l in range(i + 1, DEPTH))
        w1 = mlp_w1[i].astype(BF16)
        w2 = mlp_w2[i].astype(BF16)
        if attn_layer:
            lam_init = 0.8 - 0.6 * math.exp(-0.3 * i)
            g_head = da_subln_g[j].reshape(1, HEAD_W)
            w_qkv = da_w_qkv[j].astype(BF16)
            w_u = da_w_o[j].astype(BF16)
            q, k, vt = _qkv_project(lat[0], mod, norm_g, w_qkv, rope_tab, i, lat[1], lat[2])
            qc, kc, vtc = _qkv_project(con[0], mod, norm_g, w_qkv, rope_tab, i, con[1], con[2])
            attend = functools.partial(_attention, lam_params=da_lambda[j], subln_g=g_head,
                                       lam_init=lam_init, n_batch=n_batch, ctx_len=ctx_len)
            u_lat = attend(q, kc, vtc, k=k, vt=vt)
            u_con = attend(qc, kc, vtc) if later_attn else None
        else:
            w_u = sgu_w_out[j].astype(BF16)
            gate = functools.partial(
                _sgu_gate, mod=mod, norm_g=norm_g, w_in=sgu_w_in[j].astype(BF16),
                b_in=sgu_b_in[j].reshape(1, -1), ln_g=sgu_ln_g[j].reshape(1, -1),
                ln_b=sgu_ln_b[j].reshape(1, -1), w_s=sgu_w_s[j].astype(BF16),
                b_s_t=sgu_b_s[j].T, layer=i)
            u_lat = gate(lat[0], mod_row=lat[1])
            u_con = gate(con[0], mod_row=con[1]) if later_attn else None
        lat = (_post_mixer(lat[0], u_lat, mod, norm_g, w_u, w1, w2, i, lat[1]),) + lat[1:]
        if later_attn:
            con = (_post_mixer(con[0], u_con, mod, norm_g, w_u, w1, w2, i, con[1]),) + con[1:]
    return lat[0].reshape(n_batch, seq, d)
```
